```python
import functools
import jax, jax.numpy as jnp
from jax import lax
import numpy as np

D_MODEL = 1024
BATCH = 1
SEQ = 16384
DEPTH = 1
DEC_BATCH = 128
DEC_SEQ = 1
PAST_LEN = 16384
PAGE_SIZE = 128

MLA_HEADS = 8
QK_NOPE = 64
QK_ROPE = 32
V_HEAD = 64
Q_LORA = 384
KV_LORA = 256
ROPE_THETA = 10000.0
ATTN_QBLOCK = 128
ATTN_SCALE = (QK_NOPE + QK_ROPE) ** -0.5
SSD_HEADS = 8
SSD_HEAD_DIM = 64
SSD_GROUPS = 2
SSD_STATE = 64
CONV_W = 4
SSD_CHUNK = 128
DT_MIN = 0.001
DT_MAX = 0.1
N_EXPERTS = 64
TOP_K = 8
N_EXPERT_GROUPS = 8
TOPK_GROUPS = 4
EXPERT_FF = 256
SHARED_FF = 256
ROUTED_SCALE = 2.5
MOE_BLOCK = 256
EPS = 1e-6

MLA_WIDTH = MLA_HEADS * V_HEAD
SSD_WIDTH = SSD_HEADS * SSD_HEAD_DIM
D_MIX = MLA_WIDTH + SSD_WIDTH
SSD_BC = SSD_GROUPS * SSD_STATE
SSD_CONV_DIM = SSD_WIDTH + 2 * SSD_BC
IN_COLS = Q_LORA + KV_LORA + QK_ROPE + SSD_WIDTH + SSD_CONV_DIM + SSD_HEADS
IN_SPLITS = (Q_LORA,
             Q_LORA + KV_LORA,
             Q_LORA + KV_LORA + QK_ROPE,
             Q_LORA + KV_LORA + QK_ROPE + SSD_WIDTH,
             Q_LORA + KV_LORA + QK_ROPE + SSD_WIDTH + SSD_CONV_DIM)

kernel_name = "hymba_mla_ssd_moe_adaln_step"


def rmsnorm(x, g):
    xf = x.astype(jnp.float32)
    y = xf * lax.rsqrt(jnp.mean(xf * xf, axis=-1, keepdims=True) + EPS)
    return (y * g.astype(jnp.float32)).astype(x.dtype)


def rope(x, pos):
    half = QK_ROPE // 2
    inv = ROPE_THETA ** (-jnp.arange(half, dtype=jnp.float32) / half)
    ang = pos.astype(jnp.float32)[:, None] * inv[None, :]
    ang = ang.reshape(ang.shape[0], *([1] * (x.ndim - 3)), half)
    cos, sin = jnp.cos(ang), jnp.sin(ang)
    xf = x.astype(jnp.float32)
    x1, x2 = xf[..., :half], xf[..., half:]
    return jnp.concatenate([x1 * cos - x2 * sin, x2 * cos + x1 * sin], axis=-1).astype(x.dtype)


def modulation(c, w_ada, b_ada):
    m = jax.nn.silu(c) @ w_ada + b_ada
    return jnp.split(m[:, None, :], 6, axis=-1)


def mla_attend_prompt(q_nope, q_rope, ckv, kr, w_uk, w_uv):
    B, S = q_nope.shape[:2]
    k_nope = jnp.einsum('bsl,lhd->bhsd', ckv, w_uk)
    v = jnp.einsum('bsl,lhd->bhsd', ckv, w_uv)
    nb = S // ATTN_QBLOCK
    qn = q_nope.reshape(B, nb, ATTN_QBLOCK, MLA_HEADS, QK_NOPE).transpose(1, 0, 3, 2, 4)
    qr = q_rope.reshape(B, nb, ATTN_QBLOCK, MLA_HEADS, QK_ROPE).transpose(1, 0, 3, 2, 4)
    key_pos = jnp.arange(S, dtype=jnp.int32)

    def one_block(args):
        qn_b, qr_b, i = args
        s = (jnp.einsum('bhqd,bhkd->bhqk', qn_b, k_nope)
             + jnp.einsum('bhqr,bkr->bhqk', qr_b, kr)).astype(jnp.float32) * ATTN_SCALE
        q_pos = i * ATTN_QBLOCK + jnp.arange(ATTN_QBLOCK, dtype=jnp.int32)
        s = jnp.where(key_pos[None, :] <= q_pos[:, None], s, -jnp.inf)
        p = jax.nn.softmax(s, axis=-1).astype(v.dtype)
        return jnp.einsum('bhqk,bhkd->bqhd', p, v)

    o = lax.map(one_block, (qn, qr, jnp.arange(nb, dtype=jnp.int32)))
    return o.transpose(1, 0, 2, 3, 4).reshape(B, S, MLA_HEADS, V_HEAD)


def mla_attend_sample(q_nope, q_rope, ckv, kr, cache_lat, cache_kr, page_table, w_uk, w_uv):
    T = q_nope.shape[1]
    past = page_table.shape[1] * cache_lat.shape[1]
    q_lat = jnp.einsum('bthd,lhd->bthl', q_nope, w_uk)
    key_idx = jnp.arange(past + T, dtype=jnp.int32)
    q_idx = past + jnp.arange(T, dtype=jnp.int32)
    mask = key_idx[None, :] <= q_idx[:, None]

    def one_seq(args):
        ql, qr, pt, lat_new, kr_new = args
        lat = jnp.concatenate([cache_lat[pt].reshape(past, KV_LORA), lat_new], axis=0)
        krs = jnp.concatenate([cache_kr[pt].reshape(past, QK_ROPE), kr_new], axis=0)
        s = (jnp.einsum('thl,sl->hts', ql, lat)
             + jnp.einsum('thr,sr->hts', qr, krs)).astype(jnp.float32) * ATTN_SCALE
        s = jnp.where(mask[None], s, -jnp.inf)
        p = jax.nn.softmax(s, axis=-1).astype(lat.dtype)
        return jnp.einsum('hts,sl->thl', p, lat)

    o_lat = lax.map(one_seq, (q_lat, q_rope, page_table, ckv, kr))
    return jnp.einsum('bthl,lhd->bthd', o_lat, w_uv)


def causal_conv(xbc, buf, w, b):
    L = xbc.shape[1]
    full = jnp.concatenate([buf.astype(xbc.dtype), xbc], axis=1)
    y = b
    for k in range(CONV_W):
        y = y + full[:, k:k + L] * w[k]
    return jax.nn.silu(y), full[:, full.shape[1] - (CONV_W - 1):]


def ssd_chunked(xh, dt, A, Bm, Cm, h0):
    Bsz, L = xh.shape[:2]
    Q = min(SSD_CHUNK, L)
    Lp = -(-L // Q) * Q
    pad = Lp - L
    if pad:
        padf = lambda a: jnp.pad(a, [(0, 0), (0, pad)] + [(0, 0)] * (a.ndim - 2))
        xh, dt, Bm, Cm = padf(xh), padf(dt), padf(Bm), padf(Cm)
    nc = Lp // Q
    rep = SSD_HEADS // SSD_GROUPS
    Bh = jnp.repeat(Bm, rep, axis=2).reshape(Bsz, nc, Q, SSD_HEADS, SSD_STATE)
    Ch = jnp.repeat(Cm, rep, axis=2).reshape(Bsz, nc, Q, SSD_HEADS, SSD_STATE)
    xdt = (xh * dt[..., None]).reshape(Bsz, nc, Q, SSD_HEADS, SSD_HEAD_DIM)
    a_cum = jnp.cumsum((dt * A).reshape(Bsz, nc, Q, SSD_HEADS), axis=2)
    seg = a_cum[:, :, :, None, :] - a_cum[:, :, None, :, :]
    causal = jnp.tril(jnp.ones((Q, Q), dtype=bool))
    decay = jnp.exp(jnp.where(causal[None, None, :, :, None], seg, -jnp.inf))
    scores = jnp.einsum('bclhn,bcshn->bclsh', Ch, Bh) * decay
    y_diag = jnp.einsum('bclsh,bcshp->bclhp', scores, xdt)
    decay_to_end = jnp.exp(a_cum[:, :, -1:, :] - a_cum)
    chunk_states = jnp.einsum('bclhn,bclh,bclhp->bchpn', Bh, decay_to_end, xdt)
    chunk_decay = jnp.exp(a_cum[:, :, -1, :])

    def step(h, inp):
        st, dcy = inp
        return h * dcy[:, :, None, None] + st, h

    h_final, h_prev = lax.scan(step, h0, (chunk_states.transpose(1, 0, 2, 3, 4),
                                          chunk_decay.transpose(1, 0, 2)))
    h_prev = h_prev.transpose(1, 0, 2, 3, 4)
    y_off = jnp.einsum('bclhn,bchpn,bclh->bclhp', Ch, h_prev, jnp.exp(a_cum))
    y = (y_diag + y_off).reshape(Bsz, Lp, SSD_HEADS, SSD_HEAD_DIM)[:, :L]
    return y, h_final


def token_mixer(h, pos, conv_buf, h0, attend, p):
    B, L, _ = h.shape
    proj = h @ p['w_in']
    cq, ckv, kr, z, xbc, dt_raw = jnp.split(proj, IN_SPLITS, axis=-1)
    cq = rmsnorm(cq, p['q_norm'])
    q = jnp.einsum('blr,rhd->blhd', cq, p['w_uq'])
    q_nope = q[..., :QK_NOPE]
    q_rope = rope(q[..., QK_NOPE:], pos)
    ckv = rmsnorm(ckv, p['kv_norm'])
    kr = rope(kr, pos)
    attn = attend(q_nope, q_rope, ckv, kr).reshape(B, L, MLA_WIDTH)
    attn = rmsnorm(attn, p['attn_out_norm'])
    xbc, new_buf = causal_conv(xbc, conv_buf, p['conv_w'], p['conv_b'])
    xs, Bm, Cm = jnp.split(xbc, (SSD_WIDTH, SSD_WIDTH + SSD_BC), axis=-1)
    dt = jax.nn.softplus((dt_raw + p['dt_bias']).astype(jnp.float32))
    A = -jnp.exp(p['a_log'].astype(jnp.float32))
    xh = xs.reshape(B, L, SSD_HEADS, SSD_HEAD_DIM).astype(jnp.float32)
    y, h_final = ssd_chunked(xh, dt, A,
                             Bm.reshape(B, L, SSD_GROUPS, SSD_STATE).astype(jnp.float32),
                             Cm.reshape(B, L, SSD_GROUPS, SSD_STATE).astype(jnp.float32),
                             h0.astype(jnp.float32))
    y = y + p['d_skip'].astype(jnp.float32)[:, None] * xh
    y = y.reshape(B, L, SSD_WIDTH).astype(h.dtype) * jax.nn.silu(z)
    gs = SSD_WIDTH // SSD_GROUPS
    y = rmsnorm(y.reshape(B, L, SSD_GROUPS, gs), p['ssd_norm'].reshape(SSD_GROUPS, gs)).reshape(B, L, SSD_WIDTH)
    out = jnp.concatenate([attn, y], axis=-1) @ p['w_out']
    return out, ckv, kr, h_final.astype(h.dtype), new_buf


def moe(h, p):
    Bsz, L, D = h.shape
    t = h.reshape(-1, D)
    T = t.shape[0]
    blk = min(MOE_BLOCK, T)
    Tp = -(-T // blk) * blk
    t = jnp.pad(t, ((0, Tp - T), (0, 0)))
    rows = jnp.arange(blk)[:, None]

    def one_block(tb):
        s = jax.nn.sigmoid((tb @ p['w_router']).astype(jnp.float32))
        sel = s + p['router_bias'].astype(jnp.float32)
        grp = sel.reshape(blk, N_EXPERT_GROUPS, N_EXPERTS // N_EXPERT_GROUPS)
        grp_score = lax.top_k(grp, 2)[0].sum(-1)
        _, gidx = lax.top_k(grp_score, TOPK_GROUPS)
        gmask = jnp.zeros((blk, N_EXPERT_GROUPS), dtype=bool).at[rows, gidx].set(True)
        emask = jnp.repeat(gmask, N_EXPERTS // N_EXPERT_GROUPS, axis=1)
        _, eidx = lax.top_k(jnp.where(emask, sel, -jnp.inf), TOP_K)
        w_sel = jnp.take_along_axis(s, eidx, axis=1)
        w_sel = w_sel / (w_sel.sum(-1, keepdims=True) + 1e-20) * ROUTED_SCALE
        gates = jnp.zeros((blk, N_EXPERTS), jnp.float32).at[rows, eidx].set(w_sel)
        hid = jax.nn.silu(jnp.einsum('td,edf->tef', tb, p['w_exp_gate'])) * jnp.einsum('td,edf->tef', tb, p['w_exp_up'])
        routed = jnp.einsum('tef,efd->td', hid * gates[..., None].astype(hid.dtype), p['w_exp_down'])
        shared = (jax.nn.silu(tb @ p['w_sh_gate']) * (tb @ p['w_sh_up'])) @ p['w_sh_down']
        return routed + shared

    out = lax.map(one_block, t.reshape(Tp // blk, blk, D)).reshape(Tp, D)[:T]
    return out.reshape(Bsz, L, D)


def layer(x, c, pos, conv_buf, h0, attend, p):
    sh1, sc1, g1, sh2, sc2, g2 = modulation(c, p['w_ada'], p['b_ada'])
    h = rmsnorm(x, p['norm_pre_mix']) * (1 + sc1) + sh1
    mix, ckv, kr, h_final, new_buf = token_mixer(h, pos, conv_buf, h0, attend, p)
    x = x + g1 * rmsnorm(mix, p['norm_post_mix'])
    h = rmsnorm(x, p['norm_pre_ffn']) * (1 + sc2) + sh2
    x = x + g2 * rmsnorm(moe(h, p), p['norm_post_ffn'])
    return x, ckv, kr, h_final, new_buf


def setup_inputs(seed: int = 0) -> dict:
    key = jax.random.key(seed)
    ks = iter(jax.random.split(key, 64))
    f32 = jnp.float32
    nrm = lambda shape, scale: jax.random.normal(next(ks), shape, f32) * scale
    gain = lambda shape: 1.0 + 0.05 * jax.random.normal(next(ks), shape, f32)
    n_pages = PAST_LEN // PAGE_SIZE
    n_used = DEC_BATCH * n_pages
    n_pool = (5 * n_used) // 4
    page_table = jax.random.permutation(next(ks), n_pool)[:n_used].reshape(DEC_BATCH, n_pages).astype(jnp.int32)
    dt0 = jnp.exp(jax.random.uniform(next(ks), (DEPTH, SSD_HEADS), f32, np.log(DT_MIN), np.log(DT_MAX)))
    dt_bias = dt0 + jnp.log(-jnp.expm1(-dt0))
    a_log = jnp.log(jax.random.uniform(next(ks), (DEPTH, SSD_HEADS), f32, 1.0, 16.0))
    return {
        'x_prompt': nrm((BATCH, SEQ, D_MODEL), 1.0),
        'x_sample': nrm((DEC_BATCH, DEC_SEQ, D_MODEL), 1.0),
        'c_prompt': nrm((BATCH, D_MODEL), 1.0),
        'c_sample': nrm((DEC_BATCH, D_MODEL), 1.0),
        'cache_kv_latent': nrm((DEPTH, n_pool, PAGE_SIZE, KV_LORA), 1.0),
        'cache_k_rope': nrm((DEPTH, n_pool, PAGE_SIZE, QK_ROPE), 1.0),
        'page_table': page_table,
        'state_ssm': nrm((DEPTH, DEC_BATCH, SSD_HEADS, SSD_HEAD_DIM, SSD_STATE), 0.5),
        'state_conv': nrm((DEPTH, DEC_BATCH, CONV_W - 1, SSD_CONV_DIM), 1.0),
        'w_ada': nrm((DEPTH, D_MODEL, 6 * D_MODEL), 0.5 * D_MODEL ** -0.5),
        'b_ada': nrm((DEPTH, 6 * D_MODEL), 0.02),
        'norm_pre_mix': gain((DEPTH, D_MODEL)),
        'norm_post_mix': gain((DEPTH, D_MODEL)),
        'norm_pre_ffn': gain((DEPTH, D_MODEL)),
        'norm_post_ffn': gain((DEPTH, D_MODEL)),
        'w_in': nrm((DEPTH, D_MODEL, IN_COLS), D_MODEL ** -0.5),
        'q_norm': gain((DEPTH, Q_LORA)),
        'w_uq': nrm((DEPTH, Q_LORA, MLA_HEADS, QK_NOPE + QK_ROPE), Q_LORA ** -0.5),
        'kv_norm': gain((DEPTH, KV_LORA)),
        'w_uk': nrm((DEPTH, KV_LORA, MLA_HEADS, QK_NOPE), KV_LORA ** -0.5),
        'w_uv': nrm((DEPTH, KV_LORA, MLA_HEADS, V_HEAD), KV_LORA ** -0.5),
        'attn_out_norm': gain((DEPTH, MLA_WIDTH)),
        'conv_w': nrm((DEPTH, CONV_W, SSD_CONV_DIM), CONV_W ** -0.5),
        'conv_b': nrm((DEPTH, SSD_CONV_DIM), 0.02),
        'dt_bias': dt_bias,
        'a_log': a_log,
        'd_skip': gain((DEPTH, SSD_HEADS)),
        'ssd_norm': gain((DEPTH, SSD_WIDTH)),
        'w_out': nrm((DEPTH, D_MIX, D_MODEL), D_MIX ** -0.5),
        'w_router': nrm((DEPTH, D_MODEL, N_EXPERTS), D_MODEL ** -0.5),
        'router_bias': nrm((DEPTH, N_EXPERTS), 0.01),
        'w_exp_gate': nrm((DEPTH, N_EXPERTS, D_MODEL, EXPERT_FF), D_MODEL ** -0.5),
        'w_exp_up': nrm((DEPTH, N_EXPERTS, D_MODEL, EXPERT_FF), D_MODEL ** -0.5),
        'w_exp_down': nrm((DEPTH, N_EXPERTS, EXPERT_FF, D_MODEL), EXPERT_FF ** -0.5),
        'w_sh_gate': nrm((DEPTH, D_MODEL, SHARED_FF), D_MODEL ** -0.5),
        'w_sh_up': nrm((DEPTH, D_MODEL, SHARED_FF), D_MODEL ** -0.5),
        'w_sh_down': nrm((DEPTH, SHARED_FF, D_MODEL), SHARED_FF ** -0.5),
    }


def reference(x_prompt, x_sample, c_prompt, c_sample, cache_kv_latent, cache_k_rope, page_table,
              state_ssm, state_conv, w_ada, b_ada, norm_pre_mix, norm_post_mix, norm_pre_ffn,
              norm_post_ffn, w_in, q_norm, w_uq, kv_norm, w_uk, w_uv, attn_out_norm, conv_w, conv_b,
              dt_bias, a_log, d_skip, ssd_norm, w_out, w_router, router_bias, w_exp_gate, w_exp_up,
              w_exp_down, w_sh_gate, w_sh_up, w_sh_down):
    bp, seq = x_prompt.shape[:2]
    dec_seq = x_sample.shape[1]
    past = page_table.shape[1] * cache_kv_latent.shape[2]
    pos_prompt = jnp.arange(seq, dtype=jnp.int32)
    pos_sample = past + jnp.arange(dec_seq, dtype=jnp.int32)
    conv0 = jnp.zeros((bp, CONV_W - 1, SSD_CONV_DIM), x_prompt.dtype)
    ssm0 = jnp.zeros((bp, SSD_HEADS, SSD_HEAD_DIM, SSD_STATE), jnp.float32)
    xp, xs = x_prompt, x_sample
    lat_p, kr_p, ssm_p, conv_p = [], [], [], []
    lat_s, kr_s, ssm_s, conv_s = [], [], [], []
    for l in range(DEPTH):
        p = dict(w_ada=w_ada[l], b_ada=b_ada[l], norm_pre_mix=norm_pre_mix[l],
                 norm_post_mix=norm_post_mix[l], norm_pre_ffn=norm_pre_ffn[l],
                 norm_post_ffn=norm_post_ffn[l], w_in=w_in[l], q_norm=q_norm[l], w_uq=w_uq[l],
                 kv_norm=kv_norm[l], attn_out_norm=attn_out_norm[l], conv_w=conv_w[l],
                 conv_b=conv_b[l], dt_bias=dt_bias[l], a_log=a_log[l], d_skip=d_skip[l],
                 ssd_norm=ssd_norm[l], w_out=w_out[l], w_router=w_router[l],
                 router_bias=router_bias[l], w_exp_gate=w_exp_gate[l], w_exp_up=w_exp_up[l],
                 w_exp_down=w_exp_down[l], w_sh_gate=w_sh_gate[l], w_sh_up=w_sh_up[l],
                 w_sh_down=w_sh_down[l])
        attend_p = functools.partial(mla_attend_prompt, w_uk=w_uk[l], w_uv=w_uv[l])
        attend_s = functools.partial(mla_attend_sample, cache_lat=cache_kv_latent[l],
                                     cache_kr=cache_k_rope[l], page_table=page_table,
                                     w_uk=w_uk[l], w_uv=w_uv[l])
        xp, a1, a2, a3, a4 = layer(xp, c_prompt, pos_prompt, conv0, ssm0, attend_p, p)
        xs, b1, b2, b3, b4 = layer(xs, c_sample, pos_sample, state_conv[l], state_ssm[l], attend_s, p)
        lat_p.append(a1); kr_p.append(a2); ssm_p.append(a3); conv_p.append(a4)
        lat_s.append(b1); kr_s.append(b2); ssm_s.append(b3); conv_s.append(b4)
    return (xp, xs,
            jnp.stack(lat_p), jnp.stack(kr_p), jnp.stack(ssm_p), jnp.stack(conv_p),
            jnp.stack(lat_s), jnp.stack(kr_s), jnp.stack(ssm_s), jnp.stack(conv_s))
```

```python
import functools

import jax
import jax.numpy as jnp
from jax import lax
from jax.experimental import pallas as pl
from jax.experimental.pallas import tpu as pltpu

F32 = jnp.float32
BF16 = jnp.bfloat16

EPS = 1e-6
ROPE_THETA = 10000.0
ROUTED_SCALE = 2.5
N_EXPERT_GROUPS = 8
TOPK_GROUPS = 4
TOP_K = 8
CONV_W = 4
SSD_CHUNK = 128
SSD_GROUPS = 2

LANES = 128
VMEM_LIMIT = 56 << 20


def _cparams(sem):
    return pltpu.CompilerParams(dimension_semantics=sem, vmem_limit_bytes=VMEM_LIMIT)


def _dot(a, b):
    return jnp.dot(a, b, preferred_element_type=F32)


def _dot_nt(a, b):
    return lax.dot_general(a, b, (((1,), (1,)), ((), ())), preferred_element_type=F32)


def _dot_tn(a, b):
    return lax.dot_general(a, b, (((0,), (0,)), ((), ())), preferred_element_type=F32)


def _rms(x, g):
    return x * lax.rsqrt(jnp.mean(x * x, axis=-1, keepdims=True) + EPS) * g


def _silu(x):
    return x * jax.nn.sigmoid(x)


def _softplus(x):
    return jnp.maximum(x, 0.0) + jnp.log1p(jnp.exp(-jnp.abs(x)))


def _split3(x):
    x1 = x.astype(BF16)
    r = x - x1.astype(F32)
    x2 = r.astype(BF16)
    x3 = (r - x2.astype(F32)).astype(BF16)
    return x1, x2, x3


def _rope_kernel(inv_ref, cos_ref, sin_ref, *, pos0):
    n = cos_ref.shape[1]
    pos = (lax.broadcasted_iota(jnp.int32, (inv_ref.shape[0], n), 1) + pos0).astype(F32)
    ang = pos * inv_ref[...]
    cos_ref[...] = jnp.cos(ang)
    sin_ref[...] = jnp.sin(ang)


def _rope_tables(pos0, n, half):
    inv = (ROPE_THETA ** (-jnp.arange(half, dtype=F32) / half)).reshape(half, 1)
    cos_t, sin_t = pl.pallas_call(
        functools.partial(_rope_kernel, pos0=pos0),
        out_shape=(jax.ShapeDtypeStruct((half, n), F32),) * 2,
        name="rope_tables",
    )(inv)
    return cos_t, sin_t


def _head_slot_tables(cos_t, sin_t, m, nope, rope):
    n = cos_t.shape[1]
    c = jnp.broadcast_to(cos_t.T, (m, cos_t.shape[0])) if n == 1 else cos_t.T
    s = jnp.broadcast_to(sin_t.T, (m, sin_t.shape[0])) if n == 1 else sin_t.T
    pad = LANES - nope - rope
    c_slot = jnp.concatenate([jnp.ones((m, nope), F32), c, c, jnp.zeros((m, pad), F32)], axis=1)
    s_slot = jnp.concatenate([jnp.zeros((m, nope), F32), s, s, jnp.zeros((m, pad), F32)], axis=1)
    return c_slot, s_slot


def _mod_kernel(c_ref, w_ref, b_ref, o_ref):
    o_ref[...] = _dot(_silu(c_ref[...]).astype(BF16), w_ref[...]) + b_ref[...]


def _modulation(c_all, w_ada, b_ada):
    m, d = c_all.shape
    n = w_ada.shape[1]
    tn = 1536
    return pl.pallas_call(
        _mod_kernel,
        grid=(n // tn,),
        in_specs=[pl.BlockSpec((m, d), lambda j: (0, 0)),
                  pl.BlockSpec((d, tn), lambda j: (0, j)),
                  pl.BlockSpec((1, tn), lambda j: (0, j))],
        out_specs=pl.BlockSpec((m, tn), lambda j: (0, j)),
        out_shape=jax.ShapeDtypeStruct((m, n), F32),
        compiler_params=_cparams(("parallel",)),
        name="modulation",
    )(c_all, w_ada, b_ada)


def _pre_kernel(x_ref, sc_ref, sh_ref, gpre_ref, win_ref, qn_ref, wuq_ref, kvn_ref, wuk_ref, wuv_ref,
                c_ref, s_ref, *out_refs, dims, with_kv):
    ql, kvl, heads, scale = dims
    if with_kv:
        q_ref, k_ref, v_ref, ckv_ref, kr_ref, z_ref, xbc_ref, dt_ref = out_refs
    else:
        q_ref, ckv_ref, kr_ref, z_ref, xbc_ref, dt_ref = out_refs
    h = (_rms(x_ref[...], gpre_ref[...]) * (1.0 + sc_ref[...]) + sh_ref[...]).astype(BF16)
    c_slot = c_ref[...]
    s_slot = s_ref[...]
    hw = heads * LANES
    o0 = 0
    cq = _dot(h, win_ref[:, o0:o0 + ql])
    o0 += ql
    qa = _dot(_rms(cq, qn_ref[...]).astype(BF16), wuq_ref[...])
    for hh in range(heads):
        lo = hh * LANES
        qh = qa[:, lo:lo + LANES] * c_slot + qa[:, hw + lo:hw + lo + LANES] * s_slot
        q_ref[hh] = (qh * scale).astype(BF16)
    ckv = _rms(_dot(h, win_ref[:, o0:o0 + kvl]), kvn_ref[...])
    o0 += kvl
    ckv_ref[...] = ckv
    kb = _dot(h, win_ref[:, o0:o0 + 2 * LANES])
    o0 += 2 * LANES
    kr = kb[:, :LANES] * c_slot + kb[:, LANES:] * s_slot
    kr_ref[...] = kr
    if with_kv:
        ckv_b = ckv.astype(BF16)
        ka = _dot(ckv_b, wuk_ref[...])
        va = _dot(ckv_b, wuv_ref[...])
        for hh in range(heads):
            lo = hh * LANES
            k_ref[hh] = (ka[:, lo:lo + LANES] + kr).astype(BF16)
            v_ref[hh] = va[:, lo:lo + LANES].astype(BF16)
    nz = z_ref.shape[1]
    z_ref[...] = _dot(h, win_ref[:, o0:o0 + nz])
    o0 += nz
    nx = xbc_ref.shape[1]
    xbc_ref[...] = _dot(h, win_ref[:, o0:o0 + nx])
    o0 += nx
    dt_ref[...] = _dot(h, win_ref[:, o0:o0 + LANES])


def _pre_mix(x, sc, sh, gpre, win, qn, wuq, kvn, wuk, wuv, c_slot, s_slot, *, tm, dims, with_kv, widths):
    m, d = x.shape
    ql, kvl, heads, _ = dims
    nz, nx = widths
    mm = sc.shape[0]
    row = lambda i: (i, 0)
    fix = lambda i: (0, 0)
    mod_spec = pl.BlockSpec((tm, d), row) if mm == m else pl.BlockSpec((1, d), fix)
    head_spec = pl.BlockSpec((heads, tm, LANES), lambda i: (0, i, 0))
    head_shape = jax.ShapeDtypeStruct((heads, m, LANES), BF16)
    out_specs = [head_spec]
    out_shape = [head_shape]
    if with_kv:
        out_specs += [head_spec, head_spec]
        out_shape += [head_shape, head_shape]
    out_specs += [pl.BlockSpec((tm, kvl), row), pl.BlockSpec((tm, LANES), row), pl.BlockSpec((tm, nz), row),
                  pl.BlockSpec((tm, nx), row), pl.BlockSpec((tm, LANES), row)]
    out_shape += [jax.ShapeDtypeStruct((m, kvl), F32), jax.ShapeDtypeStruct((m, LANES), F32),
                  jax.ShapeDtypeStruct((m, nz), F32), jax.ShapeDtypeStruct((m, nx), F32),
                  jax.ShapeDtypeStruct((m, LANES), F32)]
    full = lambda a: pl.BlockSpec(a.shape, fix)
    return pl.pallas_call(
        functools.partial(_pre_kernel, dims=dims, with_kv=with_kv),
        grid=(m // tm,),
        in_specs=[pl.BlockSpec((tm, d), row), mod_spec, mod_spec, full(gpre), full(win), full(qn), full(wuq),
                  full(kvn), full(wuk), full(wuv), pl.BlockSpec((tm, LANES), row), pl.BlockSpec((tm, LANES), row)],
        out_specs=out_specs,
        out_shape=out_shape,
        compiler_params=_cparams(("parallel",)),
        name="pre_mix_kv" if with_kv else "pre_mix",
    )(x, sc, sh, gpre, win, qn, wuq, kvn, wuk, wuv, c_slot, s_slot)


def _attn_kernel(q_ref, k_ref, v_ref, o_ref, m_sc, l_sc, acc_sc):
    qi = pl.program_id(1)
    ki = pl.program_id(2)

    @pl.when(ki == 0)
    def _():
        m_sc[...] = jnp.full(m_sc.shape, -jnp.inf, F32)
        l_sc[...] = jnp.zeros(l_sc.shape, F32)
        acc_sc[...] = jnp.zeros(acc_sc.shape, F32)

    def step(diagonal):
        s = _dot_nt(q_ref[0], k_ref[0])
        if diagonal:
            r = lax.broadcasted_iota(jnp.int32, s.shape, 0)
            c = lax.broadcasted_iota(jnp.int32, s.shape, 1)
            s = jnp.where(c <= r, s, -jnp.inf)
        m_prev = m_sc[...]
        m_new = jnp.maximum(m_prev, jnp.max(s, axis=-1, keepdims=True))
        alpha = jnp.exp(m_prev - m_new)
        p = jnp.exp(s - m_new)
        l_sc[...] = alpha * l_sc[...] + jnp.sum(p, axis=-1, keepdims=True)
        acc_sc[...] = alpha * acc_sc[...] + _dot(p.astype(BF16), v_ref[0])
        m_sc[...] = m_new

    @pl.when(ki < qi)
    def _():
        step(False)

    @pl.when(ki == qi)
    def _():
        step(True)
        o_ref[...] = (acc_sc[...] / l_sc[...]).astype(o_ref.dtype)


def _prompt_attention(q, k, v, *, tq):
    heads, s, _ = q.shape
    nq = s // tq
    kv_map = lambda h, i, j: (h, jnp.minimum(i, j), 0)
    return pl.pallas_call(
        _attn_kernel,
        grid=(heads, nq, nq),
        in_specs=[pl.BlockSpec((1, tq, LANES), lambda h, i, j: (h, i, 0)),
                  pl.BlockSpec((1, tq, LANES), kv_map),
                  pl.BlockSpec((1, tq, LANES), kv_map)],
        out_specs=pl.BlockSpec((tq, LANES), lambda h, i, j: (i, h)),
        out_shape=jax.ShapeDtypeStruct((s, heads * LANES), BF16),
        scratch_shapes=[pltpu.VMEM((tq, 1), F32), pltpu.VMEM((tq, 1), F32), pltpu.VMEM((tq, LANES), F32)],
        compiler_params=_cparams(("parallel", "parallel", "arbitrary")),
        name="prompt_attention",
    )(q, k, v)


def _ssd_kernel(z_ref, xbc_ref, dt_ref, conv0_ref, ssm0_ref, cw_ref, cb_ref, dtb_ref, alog_ref, dskip_ref,
                gn_ref, y_ref, ssm_ref, conv_ref, cat_sc, st_sc, *, heads, hd, ns):
    i = pl.program_id(0)
    q = xbc_ref.shape[0]
    width = heads * hd
    tail = CONV_W - 1

    @pl.when(i == 0)
    def _():
        cat_sc[0:8, :] = jnp.zeros((8, cat_sc.shape[1]), F32)
        cat_sc[8 - tail:8, :] = conv0_ref[...]
        st_sc[...] = ssm0_ref[...]

    xbc = xbc_ref[...]
    cat_sc[8:8 + q, :] = xbc
    acc = cb_ref[...]
    for kk in range(CONV_W):
        acc = acc + cat_sc[8 - tail + kk:8 - tail + kk + q, :] * cw_ref[kk:kk + 1, :]
    act = _silu(acc)
    cat_sc[0:8, :] = xbc[q - 8:q, :]
    conv_ref[...] = xbc[q - 8:q, :]

    xs = act[:, :width]
    gb = SSD_GROUPS * ns
    bm = act[:, width:width + gb]
    cm = act[:, width + gb:width + 2 * gb]
    dt = _softplus(dt_ref[...] + dtb_ref[...])
    a = dt * (-jnp.exp(alog_ref[...]))
    r = lax.broadcasted_iota(jnp.int32, (q, q), 0)
    c = lax.broadcasted_iota(jnp.int32, (q, q), 1)
    causal = c <= r
    tril = jnp.where(causal, 1.0, 0.0).astype(BF16)
    a1, a2, a3 = _split3(a)
    a_cum = _dot(tril, a1) + _dot(tril, a2) + _dot(tril, a3)
    a_cum_t = a_cum.T
    cb = [_dot_nt(cm[:, g * ns:(g + 1) * ns].astype(BF16), bm[:, g * ns:(g + 1) * ns].astype(BF16))
          for g in range(SSD_GROUPS)]
    rep = heads // SSD_GROUPS
    ys = []
    for hh in range(heads):
        g = hh // rep
        acol = a_cum[:, hh:hh + 1]
        arow = a_cum_t[hh:hh + 1, :]
        decay = jnp.exp(jnp.where(causal, acol - arow, -jnp.inf))
        xh = xs[:, hh * hd:(hh + 1) * hd]
        xdt = xh * dt[:, hh:hh + 1]
        y_diag = _dot((cb[g] * decay).astype(BF16), xdt.astype(BF16))
        st = st_sc[hh]
        c_g = cm[:, g * ns:(g + 1) * ns].astype(BF16)
        y_off = _dot_nt(c_g, st.astype(BF16)) * jnp.exp(acol)
        ys.append(y_diag + y_off + dskip_ref[:, hh * hd:(hh + 1) * hd] * xh)
        a_last = a_cum[q - 1:q, hh:hh + 1]
        to_end = jnp.exp(a_last - acol)
        upd = _dot_tn((xdt * to_end).astype(BF16), bm[:, g * ns:(g + 1) * ns].astype(BF16))
        st_sc[hh] = st * jnp.exp(a_last) + upd
    y = jnp.concatenate(ys, axis=1) * _silu(z_ref[...])
    gs = width // SSD_GROUPS
    y = jnp.concatenate([_rms(y[:, g * gs:(g + 1) * gs], gn_ref[:, g * gs:(g + 1) * gs])
                         for g in range(SSD_GROUPS)], axis=1)
    y_ref[...] = y.astype(y_ref.dtype)
    ssm_ref[...] = st_sc[...]


def _prompt_ssd(z, xbc, dt, conv0, ssm0, cw, cb, dtb, alog, dskip, gn, *, heads, hd, ns):
    s, width = z.shape
    nx = xbc.shape[1]
    q = SSD_CHUNK
    row = lambda i: (i, 0)
    fix2 = lambda i: (0, 0)
    fix3 = lambda i: (0, 0, 0)
    full = lambda a: pl.BlockSpec(a.shape, fix2 if a.ndim == 2 else fix3)
    return pl.pallas_call(
        functools.partial(_ssd_kernel, heads=heads, hd=hd, ns=ns),
        grid=(s // q,),
        in_specs=[pl.BlockSpec((q, width), row), pl.BlockSpec((q, nx), row), pl.BlockSpec((q, LANES), row),
                  full(conv0), full(ssm0), full(cw), full(cb), full(dtb), full(alog), full(dskip), full(gn)],
        out_specs=[pl.BlockSpec((q, width), row), pl.BlockSpec((heads, hd, ns), fix3),
                   pl.BlockSpec((8, nx), fix2)],
        out_shape=[jax.ShapeDtypeStruct((s, width), BF16), jax.ShapeDtypeStruct((heads, hd, ns), F32),
                   jax.ShapeDtypeStruct((8, nx), F32)],
        scratch_shapes=[pltpu.VMEM((8 + q, nx), F32), pltpu.VMEM((heads, hd, ns), F32)],
        compiler_params=_cparams(("arbitrary",)),
        name="prompt_ssd",
    )(z, xbc, dt, conv0, ssm0, cw, cb, dtb, alog, dskip, gn)


def _ssd_step_kernel(z_ref, xbc_ref, dt_ref, conv_ref, ssm_ref, cw_ref, cb_ref, dtb_ref, alog_ref, dskip_ref,
                     gn_ref, y_ref, ssm_out, conv_out, *, heads, hd, ns):
    width = heads * hd
    cs = conv_ref[0]
    xn = xbc_ref[0]
    acc = cb_ref[...] + xn * cw_ref[CONV_W - 1:CONV_W, :]
    for kk in range(CONV_W - 1):
        acc = acc + cs[kk:kk + 1, :] * cw_ref[kk:kk + 1, :]
    act = _silu(acc)
    for kk in range(CONV_W - 2):
        conv_out[0, kk:kk + 1, :] = cs[kk + 1:kk + 2, :]
    conv_out[0, CONV_W - 2:CONV_W - 1, :] = xn

    xs = act[:, :width]
    gb = SSD_GROUPS * ns
    bv = act[:, width:width + gb]
    cv = act[:, width + gb:width + 2 * gb]
    dt = _softplus(dt_ref[0] + dtb_ref[...])
    da = jnp.exp(dt * (-jnp.exp(alog_ref[...])))
    rows = heads * hd
    head_of_lane = lax.broadcasted_iota(jnp.int32, (1, width), 1) // hd
    head_of_row = lax.broadcasted_iota(jnp.int32, (rows, 1), 0) // hd
    dt_lane = jnp.zeros((1, width), F32)
    da_col = jnp.zeros((rows, 1), F32)
    for hh in range(heads):
        dt_lane = jnp.where(head_of_lane == hh, dt[:, hh:hh + 1], dt_lane)
        da_col = jnp.where(head_of_row == hh, da[:, hh:hh + 1], da_col)
    xdt = xs * dt_lane
    ri = lax.broadcasted_iota(jnp.int32, (rows, width), 0)
    ci = lax.broadcasted_iota(jnp.int32, (rows, width), 1)
    diag = ri == ci
    xdt_col = jnp.sum(jnp.where(diag, xdt, 0.0), axis=1, keepdims=True)
    rep = heads // SSD_GROUPS
    grp_of_row = lax.broadcasted_iota(jnp.int32, (rows, ns), 0) // (rep * hd)
    b_sel = jnp.zeros((rows, ns), F32)
    c_sel = jnp.zeros((rows, ns), F32)
    for g in range(SSD_GROUPS):
        b_sel = jnp.where(grp_of_row == g, bv[:, g * ns:(g + 1) * ns], b_sel)
        c_sel = jnp.where(grp_of_row == g, cv[:, g * ns:(g + 1) * ns], c_sel)
    st = ssm_ref[0].reshape(rows, ns)
    st_new = st * da_col + xdt_col * b_sel
    ssm_out[0] = st_new.reshape(heads, hd, ns)
    y_col = jnp.sum(st_new * c_sel, axis=1, keepdims=True)
    y = jnp.sum(jnp.where(diag, y_col, 0.0), axis=0, keepdims=True)
    y = (y + dskip_ref[...] * xs) * _silu(z_ref[0])
    gs = width // SSD_GROUPS
    y = jnp.concatenate([_rms(y[:, g * gs:(g + 1) * gs], gn_ref[:, g * gs:(g + 1) * gs])
                         for g in range(SSD_GROUPS)], axis=1)
    y_ref[0] = y.astype(y_ref.dtype)


def _sample_ssd(z, xbc, dt, conv, ssm, cw, cb, dtb, alog, dskip, gn, *, heads, hd, ns):
    b, width = z.shape
    nx = xbc.shape[1]
    per3 = lambda i: (i, 0, 0)
    per4 = lambda i: (i, 0, 0, 0)
    fix2 = lambda i: (0, 0)
    full = lambda a: pl.BlockSpec(a.shape, fix2)
    y, ssm_new, conv_new = pl.pallas_call(
        functools.partial(_ssd_step_kernel, heads=heads, hd=hd, ns=ns),
        grid=(b,),
        in_specs=[pl.BlockSpec((1, 1, width), per3), pl.BlockSpec((1, 1, nx), per3),
                  pl.BlockSpec((1, 1, LANES), per3), pl.BlockSpec((1, CONV_W - 1, nx), per3),
                  pl.BlockSpec((1, heads, hd, ns), per4),
                  full(cw), full(cb), full(dtb), full(alog), full(dskip), full(gn)],
        out_specs=[pl.BlockSpec((1, 1, width), per3), pl.BlockSpec((1, heads, hd, ns), per4),
                   pl.BlockSpec((1, CONV_W - 1, nx), per3)],
        out_shape=[jax.ShapeDtypeStruct((b, 1, width), BF16), jax.ShapeDtypeStruct(ssm.shape, F32),
                   jax.ShapeDtypeStruct(conv.shape, F32)],
        compiler_params=_cparams(("parallel",)),
        name="sample_ssd",
    )(z.reshape(b, 1, width), xbc.reshape(b, 1, nx), dt.reshape(b, 1, LANES), conv, ssm,
      cw, cb, dtb, alog, dskip, gn)
    return y.reshape(b, width), ssm_new, conv_new


def _absorb_kernel(q_ref, w_ref, o_ref):
    o_ref[0] = _dot(q_ref[0], w_ref[0]).astype(o_ref.dtype)


def _absorb_queries(q, w_abs):
    heads, b, _ = q.shape
    n = w_abs.shape[2]
    per = lambda h: (h, 0, 0)
    return pl.pallas_call(
        _absorb_kernel,
        grid=(heads,),
        in_specs=[pl.BlockSpec((1, b, LANES), per), pl.BlockSpec((1, LANES, n), per)],
        out_specs=pl.BlockSpec((1, b, n), per),
        out_shape=jax.ShapeDtypeStruct((heads, b, n), BF16),
        compiler_params=_cparams(("parallel",)),
        name="absorb_queries",
    )(q, w_abs)


def _paged_attn_kernel(pt_ref, q_ref, lnew_ref, knew_ref, lat_hbm, kr_hbm, o_ref,
                       lat_buf, kr_buf, sem, m_sc, l_sc, acc_sc, *, pages_per_step, kvl, rope):
    b = pl.program_id(0)
    c = pl.program_id(1)
    nb = pl.num_programs(0)
    nc = pl.num_programs(1)
    t = b * nc + c
    slot = t % 2
    page = lat_hbm.shape[1]

    def copies(bb, cc, sl):
        out = []
        for j in range(pages_per_step):
            pg = pt_ref[bb, cc * pages_per_step + j]
            out.append(pltpu.make_async_copy(lat_hbm.at[pg], lat_buf.at[sl, pl.ds(j * page, page)], sem.at[0, sl]))
            out.append(pltpu.make_async_copy(kr_hbm.at[pg], kr_buf.at[sl, pl.ds(j * page, page)], sem.at[1, sl]))
        return out

    @pl.when(t == 0)
    def _():
        for cp in copies(b, c, slot):
            cp.start()

    @pl.when(t + 1 < nb * nc)
    def _():
        wrap = c + 1 == nc
        nb_ = jnp.where(wrap, b + 1, b)
        nc_ = jnp.where(wrap, 0, c + 1)
        for cp in copies(nb_, nc_, 1 - slot):
            cp.start()

    @pl.when(c == 0)
    def _():
        m_sc[...] = jnp.full(m_sc.shape, -jnp.inf, F32)
        l_sc[...] = jnp.zeros(l_sc.shape, F32)
        acc_sc[...] = jnp.zeros(acc_sc.shape, F32)

    for cp in copies(b, c, slot):
        cp.wait()

    q = q_ref[0]
    q_lat = q[:, :kvl]
    q_rope = q[:, kvl:kvl + rope]
    lat = lat_buf[slot].astype(BF16)
    kr = kr_buf[slot].astype(BF16)
    s = _dot_nt(q_lat, lat) + _dot_nt(q_rope, kr)
    m_prev = m_sc[...]
    m_new = jnp.maximum(m_prev, jnp.max(s, axis=-1, keepdims=True))
    alpha = jnp.exp(m_prev - m_new)
    p = jnp.exp(s - m_new)
    l_sc[...] = alpha * l_sc[...] + jnp.sum(p, axis=-1, keepdims=True)
    acc_sc[...] = alpha * acc_sc[...] + _dot(p.astype(BF16), lat)
    m_sc[...] = m_new

    @pl.when(c == nc - 1)
    def _():
        lnew = lnew_ref[0]
        knew = knew_ref[0]
        s_new = (jnp.sum(q_lat.astype(F32) * lnew, axis=-1, keepdims=True)
                 + jnp.sum(q_rope.astype(F32) * knew, axis=-1, keepdims=True))
        m_prev = m_sc[...]
        m_fin = jnp.maximum(m_prev, s_new)
        alpha = jnp.exp(m_prev - m_fin)
        p_new = jnp.exp(s_new - m_fin)
        l_fin = alpha * l_sc[...] + p_new
        o_ref[0] = (alpha * acc_sc[...] + p_new * lnew) / l_fin


def _paged_attention(page_table, q_abs, lat_new, kr_new, cache_lat, cache_kr, *, pages_per_step):
    b, heads, qw = q_abs.shape
    n_pages = page_table.shape[1]
    page, kvl = cache_lat.shape[1:]
    rope = cache_kr.shape[2]
    nc = n_pages // pages_per_step
    rows = pages_per_step * page
    per = lambda i, j, pt: (i, 0, 0)
    grid_spec = pltpu.PrefetchScalarGridSpec(
        num_scalar_prefetch=1,
        grid=(b, nc),
        in_specs=[pl.BlockSpec((1, heads, qw), per), pl.BlockSpec((1, 1, kvl), per), pl.BlockSpec((1, 1, rope), per),
                  pl.BlockSpec(memory_space=pl.ANY), pl.BlockSpec(memory_space=pl.ANY)],
        out_specs=pl.BlockSpec((1, heads, kvl), per),
        scratch_shapes=[pltpu.VMEM((2, rows, kvl), F32), pltpu.VMEM((2, rows, rope), F32),
                        pltpu.SemaphoreType.DMA((2, 2)),
                        pltpu.VMEM((heads, 1), F32), pltpu.VMEM((heads, 1), F32), pltpu.VMEM((heads, kvl), F32)],
    )
    return pl.pallas_call(
        functools.partial(_paged_attn_kernel, pages_per_step=pages_per_step, kvl=kvl, rope=rope),
        grid_spec=grid_spec,
        out_shape=jax.ShapeDtypeStruct((b, heads, kvl), F32),
        compiler_params=_cparams(("arbitrary", "arbitrary")),
        name="paged_attention",
    )(page_table, q_abs, lat_new.reshape(b, 1, kvl), kr_new.reshape(b, 1, rope), cache_lat, cache_kr)


def _value_up_kernel(o_ref, w_ref, a_ref):
    a_ref[...] = _dot(o_ref[0].astype(BF16), w_ref[0]).astype(a_ref.dtype)


def _value_up(o_lat, wuv_slots):
    heads, b, kvl = o_lat.shape
    per = lambda h: (h, 0, 0)
    return pl.pallas_call(
        _value_up_kernel,
        grid=(heads,),
        in_specs=[pl.BlockSpec((1, b, kvl), per), pl.BlockSpec((1, kvl, LANES), per)],
        out_specs=pl.BlockSpec((b, LANES), lambda h: (0, h)),
        out_shape=jax.ShapeDtypeStruct((b, heads * LANES), BF16),
        compiler_params=_cparams(("parallel",)),
        name="value_up",
    )(o_lat, wuv_slots)


def _post_kernel(x_ref, attn_ref, y_ref, g1_ref, sc_ref, sh_ref, an_ref, wo_a_ref, wo_y_ref, npost_ref, npre_ref,
                 wr_ref, rb_ref, x1_ref, h2_ref, gates_ref, *, attn_width):
    a = attn_ref[...].astype(F32)
    an = a * lax.rsqrt(jnp.sum(a * a, axis=-1, keepdims=True) / attn_width + EPS) * an_ref[...]
    mix = _dot(an.astype(BF16), wo_a_ref[...]) + _dot(y_ref[...], wo_y_ref[...])
    x1 = x_ref[...] + g1_ref[...] * _rms(mix, npost_ref[...])
    x1_ref[...] = x1
    h2 = _rms(x1, npre_ref[...]) * (1.0 + sc_ref[...]) + sh_ref[...]
    h_hi = h2.astype(BF16)
    h2_ref[...] = h_hi
    h_lo = (h2 - h_hi.astype(F32)).astype(BF16)
    logits = _dot_nt(wr_ref[0], h_hi) + _dot_nt(wr_ref[1], h_hi) + _dot_nt(wr_ref[0], h_lo)
    ne, tm = logits.shape
    s = jax.nn.sigmoid(logits)
    sel = s + rb_ref[...]
    per = ne // N_EXPERT_GROUPS
    neg = -jnp.inf
    sel3 = sel.reshape(N_EXPERT_GROUPS, per, tm)
    i3 = lax.broadcasted_iota(jnp.int32, sel3.shape, 1)
    top1 = jnp.max(sel3, axis=1, keepdims=True)
    first = jnp.min(jnp.where(sel3 == top1, i3, per), axis=1, keepdims=True)
    top2 = jnp.max(jnp.where(i3 == first, neg, sel3), axis=1, keepdims=True)
    gscore = (top1 + top2).reshape(N_EXPERT_GROUPS, tm)
    ig = lax.broadcasted_iota(jnp.int32, gscore.shape, 0)
    gmask = jnp.zeros(gscore.shape, F32)
    for _ in range(TOPK_GROUPS):
        best = jnp.max(gscore, axis=0, keepdims=True)
        pick = ig == jnp.min(jnp.where(gscore == best, ig, N_EXPERT_GROUPS), axis=0, keepdims=True)
        gmask = jnp.where(pick, 1.0, gmask)
        gscore = jnp.where(pick, neg, gscore)
    emask = jnp.concatenate([jnp.broadcast_to(gmask[g:g + 1, :], (per, tm)) for g in range(N_EXPERT_GROUPS)], axis=0)
    cand = jnp.where(emask > 0.5, sel, neg)
    ie = lax.broadcasted_iota(jnp.int32, cand.shape, 0)
    chosen = jnp.zeros(cand.shape, F32)
    for _ in range(TOP_K):
        best = jnp.max(cand, axis=0, keepdims=True)
        pick = ie == jnp.min(jnp.where(cand == best, ie, ne), axis=0, keepdims=True)
        chosen = jnp.where(pick, 1.0, chosen)
        cand = jnp.where(pick, neg, cand)
    w_sel = chosen * s
    gates = w_sel / (jnp.sum(w_sel, axis=0, keepdims=True) + 1e-20) * ROUTED_SCALE
    extra = jnp.where(lax.broadcasted_iota(jnp.int32, (LANES - ne, tm), 0) == 0, 1.0, 0.0)
    gates_ref[...] = jnp.concatenate([gates, extra], axis=0).T


def _post_mix(x, attn, y, g1, sc, sh, an, wo_a, wo_y, npost, npre, wr, rb, *, tm, attn_width):
    m, d = x.shape
    mm = g1.shape[0]
    row = lambda i: (i, 0)
    fix = lambda i: (0,) * 2
    full = lambda a: pl.BlockSpec(a.shape, (lambda i: (0,) * a.ndim))
    mod_spec = pl.BlockSpec((tm, d), row) if mm == m else pl.BlockSpec((1, d), fix)
    return pl.pallas_call(
        functools.partial(_post_kernel, attn_width=attn_width),
        grid=(m // tm,),
        in_specs=[pl.BlockSpec((tm, d), row), pl.BlockSpec((tm, attn.shape[1]), row),
                  pl.BlockSpec((tm, y.shape[1]), row), mod_spec, mod_spec, mod_spec,
                  full(an), full(wo_a), full(wo_y), full(npost), full(npre), full(wr), full(rb)],
        out_specs=[pl.BlockSpec((tm, d), row), pl.BlockSpec((tm, d), row), pl.BlockSpec((tm, LANES), row)],
        out_shape=[jax.ShapeDtypeStruct((m, d), F32), jax.ShapeDtypeStruct((m, d), BF16),
                   jax.ShapeDtypeStruct((m, LANES), F32)],
        compiler_params=_cparams(("parallel",)),
        name="post_mix",
    )(x, attn, y, g1, sc, sh, an, wo_a, wo_y, npost, npre, wr, rb)


def _moe_kernel(h_ref, gates_ref, wgu_ref, wd_ref, x1_ref, g2_ref, npost_ref, o_ref, acc_sc, *, ff):
    g = pl.program_id(1)

    @pl.when(g == 0)
    def _():
        acc_sc[...] = jnp.zeros(acc_sc.shape, F32)

    h = h_ref[...]
    gates = gates_ref[0]
    hids = []
    for j in range(wgu_ref.shape[0]):
        gu = _dot(h, wgu_ref[j])
        hid = _silu(gu[:, :ff]) * gu[:, ff:]
        hids.append((hid * gates[:, j:j + 1]).astype(BF16))
    acc_sc[...] += _dot(jnp.concatenate(hids, axis=1), wd_ref[0])

    @pl.when(g == pl.num_programs(1) - 1)
    def _():
        o_ref[...] = x1_ref[...] + g2_ref[...] * _rms(acc_sc[...], npost_ref[...])


def _moe(h2, gates, wgu, wd, x1, g2, npost, *, tm, per_step):
    m, d = h2.shape
    ne = wgu.shape[0]
    ff = wd.shape[1]
    ng = ne // per_step
    mm = g2.shape[0]
    wd_g = wd.reshape(ng, per_step * ff, d)
    gates_g = gates[:, :ne].reshape(m, ng, per_step).transpose(1, 0, 2)
    row = lambda i, g: (i, 0)
    fix = lambda i, g: (0, 0)
    mod_spec = pl.BlockSpec((tm, d), row) if mm == m else pl.BlockSpec((1, d), fix)
    return pl.pallas_call(
        functools.partial(_moe_kernel, ff=ff),
        grid=(m // tm, ng),
        in_specs=[pl.BlockSpec((tm, d), row), pl.BlockSpec((1, tm, per_step), lambda i, g: (g, i, 0)),
                  pl.BlockSpec((per_step, d, 2 * ff), lambda i, g: (g, 0, 0)),
                  pl.BlockSpec((1, per_step * ff, d), lambda i, g: (g, 0, 0)),
                  pl.BlockSpec((tm, d), row), mod_spec, pl.BlockSpec(npost.shape, fix)],
        out_specs=pl.BlockSpec((tm, d), row),
        out_shape=jax.ShapeDtypeStruct((m, d), F32),
        scratch_shapes=[pltpu.VMEM((tm, d), F32)],
        compiler_params=_cparams(("parallel", "arbitrary")),
        name="moe_ffn",
    )(h2, gates_g, wgu, wd_g, x1, g2, npost)


def _head_slots(w, lo_pad):
    rows, heads, n = w.shape
    out = jnp.zeros((rows, heads, LANES), w.dtype)
    out = out.at[:, :, lo_pad:lo_pad + n].set(w)
    return out.reshape(rows, heads * LANES)


def _rot_half(w):
    half = w.shape[-1] // 2
    return jnp.concatenate([-w[..., half:], w[..., :half]], axis=-1)


def kernel(x_prompt, x_sample, c_prompt, c_sample, cache_kv_latent, cache_k_rope, page_table, state_ssm, state_conv, w_ada, b_ada, norm_pre_mix, norm_post_mix, norm_pre_ffn, norm_post_ffn, w_in, q_norm, w_uq, kv_norm, w_uk, w_uv, attn_out_norm, conv_w, conv_b, dt_bias, a_log, d_skip, ssd_norm, w_out, w_router, router_bias, w_exp_gate, w_exp_up, w_exp_down, w_sh_gate, w_sh_up, w_sh_down):
    depth = w_in.shape[0]
    assert depth == 1, "one trunk layer"
    bp, seq, d = x_prompt.shape
    assert bp == 1
    db, dec_seq, _ = x_sample.shape
    assert dec_seq == 1
    ql = q_norm.shape[1]
    kvl = kv_norm.shape[1]
    _, _, heads, qk = w_uq.shape
    nope = w_uk.shape[3]
    rope = qk - nope
    vh = w_uv.shape[3]
    ssd_heads = dt_bias.shape[1]
    ssd_width = ssd_norm.shape[1]
    hd = ssd_width // ssd_heads
    nx = conv_w.shape[2]
    ns = (nx - ssd_width) // (2 * SSD_GROUPS)
    ne = w_router.shape[2]
    ff = w_exp_gate.shape[3]
    mla_width = heads * vh
    n_pages, page = page_table.shape[1], cache_kv_latent.shape[2]
    past = n_pages * page
    scale = float(qk) ** -0.5
    dims = (ql, kvl, heads, scale)

    o = 0
    w = w_in[0]
    w_cq = w[:, o:o + ql]; o += ql
    w_ckv = w[:, o:o + kvl]; o += kvl
    w_kr = w[:, o:o + rope]; o += rope
    w_z = w[:, o:o + ssd_width]; o += ssd_width
    w_xbc = w[:, o:o + nx]; o += nx
    w_dt = w[:, o:o + ssd_heads]
    slot = lambda a: jnp.pad(a, ((0, 0), (nope, LANES - nope - rope)))
    win = jnp.concatenate([w_cq, w_ckv, slot(w_kr), slot(_rot_half(w_kr)), w_z, w_xbc,
                           jnp.pad(w_dt, ((0, 0), (0, LANES - ssd_heads)))], axis=1).astype(BF16)
    uq = w_uq[0]
    uq_rot = jnp.concatenate([jnp.zeros_like(uq[..., :nope]), _rot_half(uq[..., nope:])], axis=-1)
    wuq = jnp.concatenate([_head_slots(uq, 0), _head_slots(uq_rot, 0)], axis=1).astype(BF16)
    wuk = _head_slots(w_uk[0], 0).astype(BF16)
    wuv = _head_slots(w_uv[0], 0).astype(BF16)
    w_abs = jnp.zeros((heads, LANES, kvl + LANES), F32)
    w_abs = w_abs.at[:, :nope, :kvl].set(w_uk[0].transpose(1, 2, 0))
    w_abs = w_abs.at[:, nope:nope + rope, kvl:kvl + rope].set(jnp.broadcast_to(jnp.eye(rope, dtype=F32), (heads, rope, rope)))
    w_abs = w_abs.astype(BF16)
    wuv_slots = jnp.pad(w_uv[0].transpose(1, 0, 2), ((0, 0), (0, 0), (0, LANES - vh))).astype(BF16)
    an = _head_slots(attn_out_norm[0].reshape(1, heads, vh), 0)
    wo = w_out[0]
    wo_a = jnp.pad(wo[:mla_width].reshape(heads, vh, d), ((0, 0), (0, LANES - vh), (0, 0))).reshape(heads * LANES, d).astype(BF16)
    wo_y = wo[mla_width:].astype(BF16)
    wr_t = w_router[0].T
    wr_hi = wr_t.astype(BF16)
    wr = jnp.stack([wr_hi, (wr_t - wr_hi.astype(F32)).astype(BF16)])
    rb = router_bias[0].reshape(ne, 1)
    wgu = jnp.concatenate([jnp.concatenate([w_exp_gate[0], w_exp_up[0]], axis=2),
                           jnp.concatenate([w_sh_gate[0], w_sh_up[0]], axis=1)[None]], axis=0).astype(BF16)
    wd = jnp.concatenate([w_exp_down[0], w_sh_down[0][None]], axis=0).astype(BF16)
    per_step = 5
    assert (ne + 1) % per_step == 0
    wada = w_ada[0].astype(BF16)
    gpre = norm_pre_mix
    cw = conv_w[0]
    cb = conv_b
    dtb = jnp.pad(dt_bias, ((0, 0), (0, LANES - ssd_heads)))
    alog = jnp.pad(a_log, ((0, 0), (0, LANES - ssd_heads)))
    dskip = jnp.repeat(d_skip, hd, axis=1)

    rows = bp + db
    rows_pad = -(-rows // 8) * 8
    c_all = jnp.pad(jnp.concatenate([c_prompt, c_sample], axis=0), ((0, rows_pad - rows), (0, 0)))
    mod = _modulation(c_all, wada, b_ada)
    mod_p = [mod[:bp, i * d:(i + 1) * d] for i in range(6)]
    mod_s = [mod[bp:rows, i * d:(i + 1) * d] for i in range(6)]

    def trunk(x, mods, c_slot, s_slot, tm, with_kv):
        sh1, sc1 = mods[0], mods[1]
        return _pre_mix(x, sc1, sh1, gpre, win, q_norm, wuq, kv_norm, wuk, wuv, c_slot, s_slot,
                        tm=tm, dims=dims, with_kv=with_kv, widths=(ssd_width, nx))

    def ffn(x, attn, y, mods, tm, tm_moe):
        g1, sh2, sc2, g2 = mods[2], mods[3], mods[4], mods[5]
        x1, h2, gates = _post_mix(x, attn, y, g1, sc2, sh2, an, wo_a, wo_y, norm_post_mix, norm_pre_ffn, wr, rb,
                                  tm=tm, attn_width=mla_width)
        return _moe(h2, gates, wgu, wd, x1, g2, norm_post_ffn, tm=tm_moe, per_step=per_step)

    cos_p, sin_p = _rope_tables(0, seq, rope // 2)
    c_slot, s_slot = _head_slot_tables(cos_p, sin_p, seq, nope, rope)
    xp = x_prompt[0]
    q, k, v, ckv_p, kr_p, z, xbc, dt = trunk(xp, mod_p, c_slot, s_slot, 512, True)
    attn_p = _prompt_attention(q, k, v, tq=512)
    conv0 = jnp.zeros((CONV_W - 1, nx), F32)
    ssm0 = jnp.zeros((ssd_heads, hd, ns), F32)
    y_p, ssm_p, conv_tail = _prompt_ssd(z, xbc, dt, conv0, ssm0, cw, cb, dtb, alog, dskip, ssd_norm,
                                        heads=ssd_heads, hd=hd, ns=ns)
    out_p = ffn(xp, attn_p, y_p, mod_p, 512, 512)

    cos_s, sin_s = _rope_tables(past, LANES, rope // 2)
    c_slot_s, s_slot_s = _head_slot_tables(cos_s[:, :1], sin_s[:, :1], db, nope, rope)
    xs = x_sample[:, 0]
    q_s, ckv_s, kr_s, z_s, xbc_s, dt_s = trunk(xs, mod_s, c_slot_s, s_slot_s, db, False)
    kr_s32 = kr_s[:, nope:nope + rope]
    q_abs = _absorb_queries(q_s, w_abs).transpose(1, 0, 2)
    o_lat = _paged_attention(page_table, q_abs, ckv_s, kr_s32, cache_kv_latent[0], cache_k_rope[0],
                             pages_per_step=16)
    attn_s = _value_up(o_lat.transpose(1, 0, 2), wuv_slots)
    y_s, ssm_s, conv_s = _sample_ssd(z_s, xbc_s, dt_s, state_conv[0], state_ssm[0], cw, cb, dtb, alog, dskip,
                                     ssd_norm, heads=ssd_heads, hd=hd, ns=ns)
    out_s = ffn(xs, attn_s, y_s, mod_s, db, db)

    return (out_p[None], out_s[:, None],
            ckv_p[None, None], kr_p[:, nope:nope + rope][None, None],
            ssm_p[None, None], conv_tail[8 - (CONV_W - 1):][None, None],
            ckv_s[None, :, None], kr_s32[None, :, None],
            ssm_s[None], conv_s[None])
```

```python
import functools

import jax
import jax.numpy as jnp
from jax import lax
from jax.experimental import pallas as pl
from jax.experimental.pallas import tpu as pltpu

F32 = jnp.float32
BF16 = jnp.bfloat16

EPS = 1e-6
ROPE_THETA = 10000.0
ROUTED_SCALE = 2.5
N_EXPERT_GROUPS = 8
TOPK_GROUPS = 4
TOP_K = 8
CONV_W = 4
SSD_CHUNK = 128
SSD_GROUPS = 2

LANES = 128
VMEM_LIMIT = 56 << 20


def _cparams(sem):
    return pltpu.CompilerParams(dimension_semantics=sem, vmem_limit_bytes=VMEM_LIMIT)


def _dot(a, b):
    return jnp.dot(a, b, preferred_element_type=F32)


def _dot_nt(a, b):
    return lax.dot_general(a, b, (((1,), (1,)), ((), ())), preferred_element_type=F32)


def _dot_tn(a, b):
    return lax.dot_general(a, b, (((0,), (0,)), ((), ())), preferred_element_type=F32)


def _rms(x, g):
    return x * lax.rsqrt(jnp.mean(x * x, axis=-1, keepdims=True) + EPS) * g


def _silu(x):
    return x * jax.nn.sigmoid(x)


def _softplus(x):
    return jnp.maximum(x, 0.0) + jnp.log1p(jnp.exp(-jnp.abs(x)))


def _split3(x):
    x1 = x.astype(BF16)
    r = x - x1.astype(F32)
    x2 = r.astype(BF16)
    x3 = (r - x2.astype(F32)).astype(BF16)
    return x1, x2, x3


def _rope_kernel(inv_ref, cos_ref, sin_ref, *, pos0):
    n = cos_ref.shape[1]
    pos = (lax.broadcasted_iota(jnp.int32, (inv_ref.shape[0], n), 1) + pos0).astype(F32)
    ang = pos * inv_ref[...]
    cos_ref[...] = jnp.cos(ang)
    sin_ref[...] = jnp.sin(ang)


def _rope_tables(pos0, n, half):
    inv = (ROPE_THETA ** (-jnp.arange(half, dtype=F32) / half)).reshape(half, 1)
    cos_t, sin_t = pl.pallas_call(
        functools.partial(_rope_kernel, pos0=pos0),
        out_shape=(jax.ShapeDtypeStruct((half, n), F32),) * 2,
        name="rope_tables",
    )(inv)
    return cos_t, sin_t


def _head_slot_tables(cos_t, sin_t, m, nope, rope):
    n = cos_t.shape[1]
    c = jnp.broadcast_to(cos_t.T, (m, cos_t.shape[0])) if n == 1 else cos_t.T
    s = jnp.broadcast_to(sin_t.T, (m, sin_t.shape[0])) if n == 1 else sin_t.T
    pad = LANES - nope - rope
    c_slot = jnp.concatenate([jnp.ones((m, nope), F32), c, c, jnp.zeros((m, pad), F32)], axis=1)
    s_slot = jnp.concatenate([jnp.zeros((m, nope), F32), s, s, jnp.zeros((m, pad), F32)], axis=1)
    return c_slot, s_slot


def _mod_kernel(c_ref, w_ref, b_ref, o_ref):
    o_ref[...] = _dot(_silu(c_ref[...]).astype(BF16), w_ref[...]) + b_ref[...]


def _modulation(c_all, w_ada, b_ada):
    m, d = c_all.shape
    n = w_ada.shape[1]
    tn = 1536
    return pl.pallas_call(
        _mod_kernel,
        grid=(n // tn,),
        in_specs=[pl.BlockSpec((m, d), lambda j: (0, 0)),
                  pl.BlockSpec((d, tn), lambda j: (0, j)),
                  pl.BlockSpec((1, tn), lambda j: (0, j))],
        out_specs=pl.BlockSpec((m, tn), lambda j: (0, j)),
        out_shape=jax.ShapeDtypeStruct((m, n), F32),
        compiler_params=_cparams(("parallel",)),
        name="modulation",
    )(c_all, w_ada, b_ada)


def _pre_kernel(x_ref, sc_ref, sh_ref, gpre_ref, win_ref, qn_ref, wuq_ref, kvn_ref, wuk_ref, wuv_ref,
                c_ref, s_ref, *out_refs, dims, with_kv):
    ql, kvl, heads, scale = dims
    if with_kv:
        q_ref, k_ref, v_ref, ckv_ref, kr_ref, z_ref, xbc_ref, dt_ref = out_refs
    else:
        q_ref, ckv_ref, kr_ref, z_ref, xbc_ref, dt_ref = out_refs
    h = (_rms(x_ref[...], gpre_ref[...]) * (1.0 + sc_ref[...]) + sh_ref[...]).astype(BF16)
    c_slot = c_ref[...]
    s_slot = s_ref[...]
    hw = heads * LANES
    o0 = 0
    cq = _dot(h, win_ref[:, o0:o0 + ql])
    o0 += ql
    qa = _dot(_rms(cq, qn_ref[...]).astype(BF16), wuq_ref[...])
    for hh in range(heads):
        lo = hh * LANES
        qh = qa[:, lo:lo + LANES] * c_slot + qa[:, hw + lo:hw + lo + LANES] * s_slot
        q_ref[hh] = (qh * scale).astype(BF16)
    ckv = _rms(_dot(h, win_ref[:, o0:o0 + kvl]), kvn_ref[...])
    o0 += kvl
    ckv_ref[...] = ckv
    kb = _dot(h, win_ref[:, o0:o0 + 2 * LANES])
    o0 += 2 * LANES
    kr = kb[:, :LANES] * c_slot + kb[:, LANES:] * s_slot
    kr_ref[...] = kr
    if with_kv:
        ckv_b = ckv.astype(BF16)
        ka = _dot(ckv_b, wuk_ref[...])
        for hh in range(heads):
            lo = hh * LANES
            k_ref[hh] = (ka[:, lo:lo + LANES] + kr).astype(BF16)
            v_ref[hh] = _dot_nt(wuv_ref[hh], ckv_b).astype(BF16)
    nz = z_ref.shape[1]
    z_ref[...] = _dot(h, win_ref[:, o0:o0 + nz])
    o0 += nz
    nx = xbc_ref.shape[1]
    xbc_ref[...] = _dot(h, win_ref[:, o0:o0 + nx])
    o0 += nx
    dt_ref[...] = _dot(h, win_ref[:, o0:o0 + LANES])


def _pre_mix(x, sc, sh, gpre, win, qn, wuq, kvn, wuk, wuv, c_slot, s_slot, *, tm, dims, with_kv, widths):
    m, d = x.shape
    ql, kvl, heads, _ = dims
    nz, nx = widths
    mm = sc.shape[0]
    row = lambda i: (i, 0)
    fix = lambda i: (0, 0)
    mod_spec = pl.BlockSpec((tm, d), row) if mm == m else pl.BlockSpec((1, d), fix)
    head_spec = pl.BlockSpec((heads, tm, LANES), lambda i: (0, i, 0))
    head_shape = jax.ShapeDtypeStruct((heads, m, LANES), BF16)
    out_specs = [head_spec]
    out_shape = [head_shape]
    if with_kv:
        vh = wuv.shape[1]
        out_specs += [head_spec, pl.BlockSpec((heads, vh, tm), lambda i: (0, 0, i))]
        out_shape += [head_shape, jax.ShapeDtypeStruct((heads, vh, m), BF16)]
    out_specs += [pl.BlockSpec((tm, kvl), row), pl.BlockSpec((tm, LANES), row), pl.BlockSpec((tm, nz), row),
                  pl.BlockSpec((tm, nx), row), pl.BlockSpec((tm, LANES), row)]
    out_shape += [jax.ShapeDtypeStruct((m, kvl), F32), jax.ShapeDtypeStruct((m, LANES), F32),
                  jax.ShapeDtypeStruct((m, nz), F32), jax.ShapeDtypeStruct((m, nx), F32),
                  jax.ShapeDtypeStruct((m, LANES), F32)]
    full = lambda a: pl.BlockSpec(a.shape, lambda i: (0,) * a.ndim)
    return pl.pallas_call(
        functools.partial(_pre_kernel, dims=dims, with_kv=with_kv),
        grid=(m // tm,),
        in_specs=[pl.BlockSpec((tm, d), row), mod_spec, mod_spec, full(gpre), full(win), full(qn), full(wuq),
                  full(kvn), full(wuk), full(wuv), pl.BlockSpec((tm, LANES), row), pl.BlockSpec((tm, LANES), row)],
        out_specs=out_specs,
        out_shape=out_shape,
        compiler_params=_cparams(("parallel",)),
        name="pre_mix_kv" if with_kv else "pre_mix",
    )(x, sc, sh, gpre, win, qn, wuq, kvn, wuk, wuv, c_slot, s_slot)


def _attn_kernel(qi_ref, ki_ref, q_ref, k_ref, vt_ref, o_ref, m_sc, l_sc, acc_sc):
    step = pl.program_id(0)
    qi = qi_ref[step]
    ki = ki_ref[step]
    heads, vh, _ = vt_ref.shape

    @pl.when(ki == 0)
    def _():
        m_sc[...] = jnp.full(m_sc.shape, -jnp.inf, F32)
        l_sc[...] = jnp.zeros(l_sc.shape, F32)
        acc_sc[...] = jnp.zeros(acc_sc.shape, F32)

    def update(diagonal):
        for hh in range(heads):
            st = _dot_nt(k_ref[hh], q_ref[hh])
            if diagonal:
                kr = lax.broadcasted_iota(jnp.int32, st.shape, 0)
                qc = lax.broadcasted_iota(jnp.int32, st.shape, 1)
                st = jnp.where(kr <= qc, st, -jnp.inf)
            m_prev = m_sc[hh]
            m_new = jnp.maximum(m_prev, jnp.max(st, axis=0, keepdims=True))
            alpha = jnp.exp(m_prev - m_new)
            p = jnp.exp(st - m_new)
            l_sc[hh] = alpha * l_sc[hh] + jnp.sum(p, axis=0, keepdims=True)
            acc_sc[hh] = alpha * acc_sc[hh] + _dot(vt_ref[hh], p.astype(BF16))
            m_sc[hh] = m_new

    @pl.when(ki < qi)
    def _():
        update(False)

    @pl.when(ki == qi)
    def _():
        update(True)
        for hh in range(heads):
            o_ref[hh * vh:(hh + 1) * vh, :] = (acc_sc[hh] / l_sc[hh]).astype(o_ref.dtype)


def _prompt_attention(q, k, vt, *, tq):
    heads, s, _ = q.shape
    vh = vt.shape[1]
    nq = s // tq
    pairs = [(i, j) for i in range(nq) for j in range(i + 1)]
    qi_arr = jnp.asarray([p[0] for p in pairs], jnp.int32)
    ki_arr = jnp.asarray([p[1] for p in pairs], jnp.int32)
    grid_spec = pltpu.PrefetchScalarGridSpec(
        num_scalar_prefetch=2,
        grid=(len(pairs),),
        in_specs=[pl.BlockSpec((heads, tq, LANES), lambda t, qi, ki: (0, qi[t], 0)),
                  pl.BlockSpec((heads, tq, LANES), lambda t, qi, ki: (0, ki[t], 0)),
                  pl.BlockSpec((heads, vh, tq), lambda t, qi, ki: (0, 0, ki[t]))],
        out_specs=pl.BlockSpec((heads * vh, tq), lambda t, qi, ki: (0, qi[t])),
        scratch_shapes=[pltpu.VMEM((heads, 1, tq), F32), pltpu.VMEM((heads, 1, tq), F32),
                        pltpu.VMEM((heads, vh, tq), F32)],
    )
    return pl.pallas_call(
        _attn_kernel,
        grid_spec=grid_spec,
        out_shape=jax.ShapeDtypeStruct((heads * vh, s), BF16),
        compiler_params=_cparams(("arbitrary",)),
        name="prompt_attention",
    )(qi_arr, ki_arr, q, k, vt)


def _ssd_kernel(z_ref, xbc_ref, dt_ref, conv0_ref, ssm0_ref, cw_ref, cb_ref, dtb_ref, alog_ref, dskip_ref,
                gn_ref, y_ref, ssm_ref, conv_ref, cat_sc, st_sc, *, heads, hd, ns):
    i = pl.program_id(0)
    q = xbc_ref.shape[0]
    width = heads * hd
    tail = CONV_W - 1

    @pl.when(i == 0)
    def _():
        cat_sc[0:8, :] = jnp.zeros((8, cat_sc.shape[1]), F32)
        cat_sc[8 - tail:8, :] = conv0_ref[...]
        st_sc[...] = ssm0_ref[...]

    xbc = xbc_ref[...]
    cat_sc[8:8 + q, :] = xbc
    acc = cb_ref[...]
    for kk in range(CONV_W):
        acc = acc + cat_sc[8 - tail + kk:8 - tail + kk + q, :] * cw_ref[kk:kk + 1, :]
    act = _silu(acc)
    cat_sc[0:8, :] = xbc[q - 8:q, :]
    conv_ref[...] = xbc[q - 8:q, :]

    xs = act[:, :width]
    gb = SSD_GROUPS * ns
    bm = act[:, width:width + gb]
    cm = act[:, width + gb:width + 2 * gb]
    dt = _softplus(dt_ref[...] + dtb_ref[...])
    a = dt * (-jnp.exp(alog_ref[...]))
    r = lax.broadcasted_iota(jnp.int32, (q, q), 0)
    c = lax.broadcasted_iota(jnp.int32, (q, q), 1)
    causal = c <= r
    tril = jnp.where(causal, 1.0, 0.0).astype(BF16)
    a1, a2, a3 = _split3(a)
    a_cum = _dot(tril, a1) + _dot(tril, a2) + _dot(tril, a3)
    a_cum_t = a_cum.T
    cb = [_dot_nt(cm[:, g * ns:(g + 1) * ns].astype(BF16), bm[:, g * ns:(g + 1) * ns].astype(BF16))
          for g in range(SSD_GROUPS)]
    rep = heads // SSD_GROUPS
    ys = []
    for hh in range(heads):
        g = hh // rep
        acol = a_cum[:, hh:hh + 1]
        arow = a_cum_t[hh:hh + 1, :]
        decay = jnp.exp(jnp.where(causal, acol - arow, -jnp.inf))
        xh = xs[:, hh * hd:(hh + 1) * hd]
        xdt = xh * dt[:, hh:hh + 1]
        y_diag = _dot((cb[g] * decay).astype(BF16), xdt.astype(BF16))
        st = st_sc[hh]
        c_g = cm[:, g * ns:(g + 1) * ns].astype(BF16)
        y_off = _dot_nt(c_g, st.astype(BF16)) * jnp.exp(acol)
        ys.append(y_diag + y_off + dskip_ref[:, hh * hd:(hh + 1) * hd] * xh)
        a_last = a_cum[q - 1:q, hh:hh + 1]
        to_end = jnp.exp(a_last - acol)
        upd = _dot_tn((xdt * to_end).astype(BF16), bm[:, g * ns:(g + 1) * ns].astype(BF16))
        st_sc[hh] = st * jnp.exp(a_last) + upd
    y = jnp.concatenate(ys, axis=1) * _silu(z_ref[...])
    gs = width // SSD_GROUPS
    y = jnp.concatenate([_rms(y[:, g * gs:(g + 1) * gs], gn_ref[:, g * gs:(g + 1) * gs])
                         for g in range(SSD_GROUPS)], axis=1)
    y_ref[...] = y.astype(y_ref.dtype)
    ssm_ref[...] = st_sc[...]


def _prompt_ssd(z, xbc, dt, conv0, ssm0, cw, cb, dtb, alog, dskip, gn, *, heads, hd, ns):
    s, width = z.shape
    nx = xbc.shape[1]
    q = SSD_CHUNK
    row = lambda i: (i, 0)
    fix2 = lambda i: (0, 0)
    fix3 = lambda i: (0, 0, 0)
    full = lambda a: pl.BlockSpec(a.shape, fix2 if a.ndim == 2 else fix3)
    return pl.pallas_call(
        functools.partial(_ssd_kernel, heads=heads, hd=hd, ns=ns),
        grid=(s // q,),
        in_specs=[pl.BlockSpec((q, width), row), pl.BlockSpec((q, nx), row), pl.BlockSpec((q, LANES), row),
                  full(conv0), full(ssm0), full(cw), full(cb), full(dtb), full(alog), full(dskip), full(gn)],
        out_specs=[pl.BlockSpec((q, width), row), pl.BlockSpec((heads, hd, ns), fix3),
                   pl.BlockSpec((8, nx), fix2)],
        out_shape=[jax.ShapeDtypeStruct((s, width), BF16), jax.ShapeDtypeStruct((heads, hd, ns), F32),
                   jax.ShapeDtypeStruct((8, nx), F32)],
        scratch_shapes=[pltpu.VMEM((8 + q, nx), F32), pltpu.VMEM((heads, hd, ns), F32)],
        compiler_params=_cparams(("arbitrary",)),
        name="prompt_ssd",
    )(z, xbc, dt, conv0, ssm0, cw, cb, dtb, alog, dskip, gn)


def _ssd_step_kernel(z_ref, xbc_ref, dt_ref, conv_ref, ssm_ref, cw_ref, cb_ref, dtb_ref, alog_ref, dskip_ref,
                     gn_ref, y_ref, ssm_out, conv_out, xs_sc, xdt_sc, b_sc, c_sc, da_sc, y_sc, *, heads, hd, ns):
    hh = pl.program_id(0)
    width = heads * hd
    gb = SSD_GROUPS * ns

    @pl.when(hh == 0)
    def _():
        xn = xbc_ref[...]
        acc = cb_ref[...] + xn * cw_ref[CONV_W - 1:CONV_W, :]
        for kk in range(CONV_W - 1):
            acc = acc + conv_ref[kk] * cw_ref[kk:kk + 1, :]
        for kk in range(CONV_W - 2):
            conv_out[kk] = conv_ref[kk + 1]
        conv_out[CONV_W - 2] = xn
        act = _silu(acc)
        xs = act[:, :width]
        xs_sc[...] = xs
        dt = _softplus(dt_ref[...] + dtb_ref[...])
        da_sc[...] = jnp.exp(dt * (-jnp.exp(alog_ref[...]))).T
        head_of_lane = lax.broadcasted_iota(jnp.int32, (1, width), 1) // hd
        dt_lane = jnp.zeros(xs.shape, F32)
        for h2 in range(heads):
            dt_lane = jnp.where(head_of_lane == h2, dt[:, h2:h2 + 1], dt_lane)
        xdt_sc[...] = (xs * dt_lane).T
        b_sc[...] = act[:, width:width + gb].T
        c_sc[...] = act[:, width + gb:width + 2 * gb].T

    g = hh // (heads // SSD_GROUPS)
    goff = pl.multiple_of(g * ns, ns)
    b_g = b_sc[pl.ds(goff, ns), :]
    c_g = c_sc[pl.ds(goff, ns), :]
    da = da_sc[pl.ds(hh, 1), :]
    row0 = pl.multiple_of(hh * hd, hd)

    def body(p, carry):
        xdt = xdt_sc[pl.ds(row0 + p, 1), :]
        new = ssm_ref[0, p] * da + xdt * b_g
        ssm_out[0, p] = new
        y_sc[pl.ds(row0 + p, 1), :] = jnp.sum(new * c_g, axis=0, keepdims=True)
        return carry

    lax.fori_loop(0, hd, body, 0, unroll=8)

    @pl.when(hh == heads - 1)
    def _():
        xs = xs_sc[...]
        y = (y_sc[...].T + dskip_ref[...] * xs) * _silu(z_ref[...])
        gs = width // SSD_GROUPS
        y = jnp.concatenate([_rms(y[:, k * gs:(k + 1) * gs], gn_ref[:, k * gs:(k + 1) * gs])
                             for k in range(SSD_GROUPS)], axis=1)
        y_ref[...] = y.astype(y_ref.dtype)


def _sample_ssd(z, xbc, dt, conv_t, ssm_t, cw, cb, dtb, alog, dskip, gn, *, heads, hd, ns):
    b, width = z.shape
    gb = SSD_GROUPS * ns
    full = lambda a: pl.BlockSpec(a.shape, lambda h: (0,) * a.ndim)
    per_head = pl.BlockSpec((1, hd, ns, b), lambda h: (h, 0, 0, 0))
    return pl.pallas_call(
        functools.partial(_ssd_step_kernel, heads=heads, hd=hd, ns=ns),
        grid=(heads,),
        in_specs=[full(z), full(xbc), full(dt), full(conv_t), per_head,
                  full(cw), full(cb), full(dtb), full(alog), full(dskip), full(gn)],
        out_specs=[pl.BlockSpec((b, width), lambda h: (0, 0)), per_head,
                   pl.BlockSpec(conv_t.shape, lambda h: (0, 0, 0))],
        out_shape=[jax.ShapeDtypeStruct((b, width), BF16), jax.ShapeDtypeStruct(ssm_t.shape, F32),
                   jax.ShapeDtypeStruct(conv_t.shape, F32)],
        scratch_shapes=[pltpu.VMEM((b, width), F32), pltpu.VMEM((width, b), F32), pltpu.VMEM((gb, b), F32),
                        pltpu.VMEM((gb, b), F32), pltpu.VMEM((LANES, b), F32), pltpu.VMEM((width, b), F32)],
        compiler_params=_cparams(("arbitrary",)),
        name="sample_ssd",
    )(z, xbc, dt, conv_t, ssm_t, cw, cb, dtb, alog, dskip, gn)


def _absorb_kernel(q_ref, w_ref, o_ref):
    o_ref[0] = _dot(q_ref[0], w_ref[0]).astype(o_ref.dtype)


def _absorb_queries(q, w_abs):
    heads, b, _ = q.shape
    n = w_abs.shape[2]
    per = lambda h: (h, 0, 0)
    return pl.pallas_call(
        _absorb_kernel,
        grid=(heads,),
        in_specs=[pl.BlockSpec((1, b, LANES), per), pl.BlockSpec((1, LANES, n), per)],
        out_specs=pl.BlockSpec((1, b, n), per),
        out_shape=jax.ShapeDtypeStruct((heads, b, n), BF16),
        compiler_params=_cparams(("parallel",)),
        name="absorb_queries",
    )(q, w_abs)


def _paged_attn_kernel(pt_ref, q_ref, lnew_ref, knew_ref, lat_hbm, krt_hbm, o_ref,
                       lat_buf, kr_buf, sem, m_sc, l_sc, acc_sc, *, pages_per_step, kvl, rope):
    b = pl.program_id(0)
    c = pl.program_id(1)
    nb = pl.num_programs(0)
    nc = pl.num_programs(1)
    t = b * nc + c
    slot = t % 2
    page = lat_hbm.shape[1]
    group = q_ref.shape[0]

    def copies(bb, cc, sl):
        out = []
        for g in range(group):
            for j in range(pages_per_step):
                pg = pt_ref[bb * group + g, cc * pages_per_step + j]
                out.append(pltpu.make_async_copy(lat_hbm.at[pg], lat_buf.at[sl, g, pl.ds(j * page, page)],
                                                 sem.at[0, sl]))
                out.append(pltpu.make_async_copy(krt_hbm.at[pg], kr_buf.at[sl, g, :, pl.ds(j * page, page)],
                                                 sem.at[1, sl]))
        return out

    @pl.when(t == 0)
    def _():
        for cp in copies(b, c, slot):
            cp.start()

    @pl.when(t + 1 < nb * nc)
    def _():
        wrap = c + 1 == nc
        nb_ = jnp.where(wrap, b + 1, b)
        nc_ = jnp.where(wrap, 0, c + 1)
        for cp in copies(nb_, nc_, 1 - slot):
            cp.start()

    @pl.when(c == 0)
    def _():
        m_sc[...] = jnp.full(m_sc.shape, -jnp.inf, F32)
        l_sc[...] = jnp.zeros(l_sc.shape, F32)
        acc_sc[...] = jnp.zeros(acc_sc.shape, F32)

    for cp in copies(b, c, slot):
        cp.wait()

    for g in range(group):
        q = q_ref[g]
        lat = lat_buf[slot, g].astype(BF16)
        krt = kr_buf[slot, g].astype(BF16)
        s = _dot_nt(q[:, :kvl], lat) + _dot(q[:, kvl:kvl + rope], krt)
        m_prev = m_sc[g]
        m_new = jnp.maximum(m_prev, jnp.max(s, axis=-1, keepdims=True))
        alpha = jnp.exp(m_prev - m_new)
        p = jnp.exp(s - m_new)
        l_sc[g] = alpha * l_sc[g] + jnp.sum(p, axis=-1, keepdims=True)
        acc_sc[g] = alpha * acc_sc[g] + _dot(p.astype(BF16), lat)
        m_sc[g] = m_new

    @pl.when(c == nc - 1)
    def _():
        for g in range(group):
            q = q_ref[g].astype(F32)
            lnew = lnew_ref[g]
            knew = knew_ref[g]
            s_new = (jnp.sum(q[:, :kvl] * lnew, axis=-1, keepdims=True)
                     + jnp.sum(q[:, kvl:kvl + rope] * knew, axis=-1, keepdims=True))
            m_prev = m_sc[g]
            m_fin = jnp.maximum(m_prev, s_new)
            alpha = jnp.exp(m_prev - m_fin)
            p_new = jnp.exp(s_new - m_fin)
            l_fin = alpha * l_sc[g] + p_new
            o_ref[g] = (alpha * acc_sc[g] + p_new * lnew) / l_fin


def _paged_attention(page_table, q_abs, lat_new, kr_new, cache_lat, cache_krt, *, pages_per_step, group):
    b, heads, qw = q_abs.shape
    n_pages = page_table.shape[1]
    page, kvl = cache_lat.shape[1:]
    rope = cache_krt.shape[1]
    nc = n_pages // pages_per_step
    rows = pages_per_step * page
    per = lambda i, j, pt: (i, 0, 0)
    grid_spec = pltpu.PrefetchScalarGridSpec(
        num_scalar_prefetch=1,
        grid=(b // group, nc),
        in_specs=[pl.BlockSpec((group, heads, qw), per), pl.BlockSpec((group, 1, kvl), per),
                  pl.BlockSpec((group, 1, rope), per),
                  pl.BlockSpec(memory_space=pl.ANY), pl.BlockSpec(memory_space=pl.ANY)],
        out_specs=pl.BlockSpec((group, heads, kvl), per),
        scratch_shapes=[pltpu.VMEM((2, group, rows, kvl), F32), pltpu.VMEM((2, group, rope, rows), F32),
                        pltpu.SemaphoreType.DMA((2, 2)),
                        pltpu.VMEM((group, heads, 1), F32), pltpu.VMEM((group, heads, 1), F32),
                        pltpu.VMEM((group, heads, kvl), F32)],
    )
    return pl.pallas_call(
        functools.partial(_paged_attn_kernel, pages_per_step=pages_per_step, kvl=kvl, rope=rope),
        grid_spec=grid_spec,
        out_shape=jax.ShapeDtypeStruct((b, heads, kvl), F32),
        compiler_params=_cparams(("arbitrary", "arbitrary")),
        name="paged_attention",
    )(page_table, q_abs, lat_new.reshape(b, 1, kvl), kr_new.reshape(b, 1, rope), cache_lat, cache_krt)


def _value_up_kernel(o_ref, w_ref, a_ref):
    a_ref[...] = _dot_nt(w_ref[0], o_ref[0].astype(BF16)).astype(a_ref.dtype)


def _value_up(o_lat, wuv_t):
    heads, b, kvl = o_lat.shape
    vh = wuv_t.shape[1]
    per = lambda h: (h, 0, 0)
    return pl.pallas_call(
        _value_up_kernel,
        grid=(heads,),
        in_specs=[pl.BlockSpec((1, b, kvl), per), pl.BlockSpec((1, vh, kvl), per)],
        out_specs=pl.BlockSpec((vh, b), lambda h: (h, 0)),
        out_shape=jax.ShapeDtypeStruct((heads * vh, b), BF16),
        compiler_params=_cparams(("parallel",)),
        name="value_up",
    )(o_lat, wuv_t)


def _post_kernel(x_ref, attn_ref, y_ref, g1_ref, sc_ref, sh_ref, an_ref, wo_a_ref, wo_y_ref, npost_ref, npre_ref,
                 wr_ref, rb_ref, x1_ref, h2_ref, gates_ref):
    a = attn_ref[...].astype(F32)
    an = a * lax.rsqrt(jnp.mean(a * a, axis=0, keepdims=True) + EPS) * an_ref[...]
    mix = _dot_tn(an.astype(BF16), wo_a_ref[...]) + _dot(y_ref[...], wo_y_ref[...])
    x1 = x_ref[...] + g1_ref[...] * _rms(mix, npost_ref[...])
    x1_ref[...] = x1
    h2 = _rms(x1, npre_ref[...]) * (1.0 + sc_ref[...]) + sh_ref[...]
    h_hi = h2.astype(BF16)
    h2_ref[...] = h_hi
    h_lo = (h2 - h_hi.astype(F32)).astype(BF16)
    logits = _dot_nt(wr_ref[0], h_hi) + _dot_nt(wr_ref[1], h_hi) + _dot_nt(wr_ref[0], h_lo)
    ne, tm = logits.shape
    s = jax.nn.sigmoid(logits)
    sel = s + rb_ref[...]
    per = ne // N_EXPERT_GROUPS
    neg = -jnp.inf
    sel3 = sel.reshape(N_EXPERT_GROUPS, per, tm)
    i3 = lax.broadcasted_iota(jnp.int32, sel3.shape, 1)
    top1 = jnp.max(sel3, axis=1, keepdims=True)
    first = jnp.min(jnp.where(sel3 == top1, i3, per), axis=1, keepdims=True)
    top2 = jnp.max(jnp.where(i3 == first, neg, sel3), axis=1, keepdims=True)
    gscore = (top1 + top2).reshape(N_EXPERT_GROUPS, tm)
    ig = lax.broadcasted_iota(jnp.int32, gscore.shape, 0)
    gmask = jnp.zeros(gscore.shape, F32)
    for _ in range(TOPK_GROUPS):
        best = jnp.max(gscore, axis=0, keepdims=True)
        pick = ig == jnp.min(jnp.where(gscore == best, ig, N_EXPERT_GROUPS), axis=0, keepdims=True)
        gmask = jnp.where(pick, 1.0, gmask)
        gscore = jnp.where(pick, neg, gscore)
    emask = jnp.concatenate([jnp.broadcast_to(gmask[g:g + 1, :], (per, tm)) for g in range(N_EXPERT_GROUPS)], axis=0)
    cand = jnp.where(emask > 0.5, sel, neg)
    ie = lax.broadcasted_iota(jnp.int32, cand.shape, 0)
    chosen = jnp.zeros(cand.shape, F32)
    for _ in range(TOP_K):
        best = jnp.max(cand, axis=0, keepdims=True)
        pick = ie == jnp.min(jnp.where(cand == best, ie, ne), axis=0, keepdims=True)
        chosen = jnp.where(pick, 1.0, chosen)
        cand = jnp.where(pick, neg, cand)
    w_sel = chosen * s
    gates = w_sel / (jnp.sum(w_sel, axis=0, keepdims=True) + 1e-20) * ROUTED_SCALE
    gates_ref[...] = jnp.concatenate([gates, jnp.zeros((LANES - ne, tm), F32)], axis=0).T


def _post_mix(x, attn_t, y, g1, sc, sh, an, wo_a, wo_y, npost, npre, wr, rb, *, tm):
    m, d = x.shape
    mm = g1.shape[0]
    row = lambda i: (i, 0)
    fix = lambda i: (0,) * 2
    full = lambda a: pl.BlockSpec(a.shape, (lambda i: (0,) * a.ndim))
    mod_spec = pl.BlockSpec((tm, d), row) if mm == m else pl.BlockSpec((1, d), fix)
    return pl.pallas_call(
        _post_kernel,
        grid=(m // tm,),
        in_specs=[pl.BlockSpec((tm, d), row), pl.BlockSpec((attn_t.shape[0], tm), lambda i: (0, i)),
                  pl.BlockSpec((tm, y.shape[1]), row), mod_spec, mod_spec, mod_spec,
                  full(an), full(wo_a), full(wo_y), full(npost), full(npre), full(wr), full(rb)],
        out_specs=[pl.BlockSpec((tm, d), row), pl.BlockSpec((tm, d), row), pl.BlockSpec((tm, LANES), row)],
        out_shape=[jax.ShapeDtypeStruct((m, d), F32), jax.ShapeDtypeStruct((m, d), BF16),
                   jax.ShapeDtypeStruct((m, LANES), F32)],
        compiler_params=_cparams(("parallel",)),
        name="post_mix",
    )(x, attn_t, y, g1, sc, sh, an, wo_a, wo_y, npost, npre, wr, rb)


def _moe_kernel(h_ref, gates_ref, wg_ref, wu_ref, wd_ref, wsg_ref, wsu_ref, wsd_ref, x1_ref, g2_ref, npost_ref,
                o_ref, acc_sc):
    g = pl.program_id(1)
    per_step = wg_ref.shape[0]
    h = h_ref[...]

    @pl.when(g == 0)
    def _():
        hid = _silu(_dot(h, wsg_ref[...])) * _dot(h, wsu_ref[...])
        acc_sc[...] = _dot(hid.astype(BF16), wsd_ref[...])

    gates = pltpu.roll(gates_ref[...], (LANES - g * per_step) % LANES, 1)
    hids = []
    for j in range(per_step):
        hid = _silu(_dot(h, wg_ref[j])) * _dot(h, wu_ref[j])
        hids.append((hid * gates[:, j:j + 1]).astype(BF16))
    acc_sc[...] += _dot(jnp.concatenate(hids, axis=1), wd_ref[0])

    @pl.when(g == pl.num_programs(1) - 1)
    def _():
        o_ref[...] = x1_ref[...] + g2_ref[...] * _rms(acc_sc[...], npost_ref[...])


def _moe(h2, gates, wg, wu, wd, wsg, wsu, wsd, x1, g2, npost, *, tm, per_step):
    m, d = h2.shape
    ne, _, ff = wg.shape
    ng = ne // per_step
    mm = g2.shape[0]
    wd_g = wd.reshape(ng, per_step * ff, d)
    row = lambda i, g: (i, 0)
    fix = lambda i, g: (0, 0)
    grp = lambda i, g: (g, 0, 0)
    mod_spec = pl.BlockSpec((tm, d), row) if mm == m else pl.BlockSpec((1, d), fix)
    full = lambda a: pl.BlockSpec(a.shape, fix)
    return pl.pallas_call(
        _moe_kernel,
        grid=(m // tm, ng),
        in_specs=[pl.BlockSpec((tm, d), row), pl.BlockSpec((tm, LANES), row),
                  pl.BlockSpec((per_step, d, ff), grp), pl.BlockSpec((per_step, d, ff), grp),
                  pl.BlockSpec((1, per_step * ff, d), grp), full(wsg), full(wsu), full(wsd),
                  pl.BlockSpec((tm, d), row), mod_spec, full(npost)],
        out_specs=pl.BlockSpec((tm, d), row),
        out_shape=jax.ShapeDtypeStruct((m, d), F32),
        scratch_shapes=[pltpu.VMEM((tm, d), F32)],
        compiler_params=_cparams(("parallel", "arbitrary")),
        name="moe_ffn",
    )(h2, gates, wg, wu, wd_g, wsg, wsu, wsd, x1, g2, npost)


def _head_slots(w, lo_pad):
    rows, heads, n = w.shape
    out = jnp.zeros((rows, heads, LANES), w.dtype)
    out = out.at[:, :, lo_pad:lo_pad + n].set(w)
    return out.reshape(rows, heads * LANES)


def _rot_half(w):
    half = w.shape[-1] // 2
    return jnp.concatenate([-w[..., half:], w[..., :half]], axis=-1)


def kernel(x_prompt, x_sample, c_prompt, c_sample, cache_kv_latent, cache_k_rope, page_table, state_ssm, state_conv, w_ada, b_ada, norm_pre_mix, norm_post_mix, norm_pre_ffn, norm_post_ffn, w_in, q_norm, w_uq, kv_norm, w_uk, w_uv, attn_out_norm, conv_w, conv_b, dt_bias, a_log, d_skip, ssd_norm, w_out, w_router, router_bias, w_exp_gate, w_exp_up, w_exp_down, w_sh_gate, w_sh_up, w_sh_down):
    depth = w_in.shape[0]
    assert depth == 1, "one trunk layer"
    bp, seq, d = x_prompt.shape
    assert bp == 1
    db, dec_seq, _ = x_sample.shape
    assert dec_seq == 1
    ql = q_norm.shape[1]
    kvl = kv_norm.shape[1]
    _, _, heads, qk = w_uq.shape
    nope = w_uk.shape[3]
    rope = qk - nope
    vh = w_uv.shape[3]
    ssd_heads = dt_bias.shape[1]
    ssd_width = ssd_norm.shape[1]
    hd = ssd_width // ssd_heads
    nx = conv_w.shape[2]
    ns = (nx - ssd_width) // (2 * SSD_GROUPS)
    ne = w_router.shape[2]
    mla_width = heads * vh
    n_pages, page = page_table.shape[1], cache_kv_latent.shape[2]
    past = n_pages * page
    scale = float(qk) ** -0.5
    dims = (ql, kvl, heads, scale)

    o = 0
    w = w_in[0]
    w_cq = w[:, o:o + ql]; o += ql
    w_ckv = w[:, o:o + kvl]; o += kvl
    w_kr = w[:, o:o + rope]; o += rope
    w_z = w[:, o:o + ssd_width]; o += ssd_width
    w_xbc = w[:, o:o + nx]; o += nx
    w_dt = w[:, o:o + ssd_heads]
    slot = lambda a: jnp.pad(a, ((0, 0), (nope, LANES - nope - rope)))
    win = jnp.concatenate([w_cq, w_ckv, slot(w_kr), slot(_rot_half(w_kr)), w_z, w_xbc,
                           jnp.pad(w_dt, ((0, 0), (0, LANES - ssd_heads)))], axis=1).astype(BF16)
    uq = w_uq[0]
    uq_rot = jnp.concatenate([jnp.zeros_like(uq[..., :nope]), _rot_half(uq[..., nope:])], axis=-1)
    wuq = jnp.concatenate([_head_slots(uq, 0), _head_slots(uq_rot, 0)], axis=1).astype(BF16)
    wuk = _head_slots(w_uk[0], 0).astype(BF16)
    wuv_t = w_uv[0].transpose(1, 2, 0).astype(BF16)
    w_abs = jnp.zeros((heads, LANES, kvl + LANES), F32)
    w_abs = w_abs.at[:, :nope, :kvl].set(w_uk[0].transpose(1, 2, 0))
    w_abs = w_abs.at[:, nope:nope + rope, kvl:kvl + rope].set(jnp.broadcast_to(jnp.eye(rope, dtype=F32), (heads, rope, rope)))
    w_abs = w_abs.astype(BF16)
    an = attn_out_norm[0].reshape(mla_width, 1)
    wo = w_out[0]
    wo_a = wo[:mla_width].astype(BF16)
    wo_y = wo[mla_width:].astype(BF16)
    wr_t = w_router[0].T
    wr_hi = wr_t.astype(BF16)
    wr = jnp.stack([wr_hi, (wr_t - wr_hi.astype(F32)).astype(BF16)])
    rb = router_bias[0].reshape(ne, 1)
    wg, wu, wd = w_exp_gate[0].astype(BF16), w_exp_up[0].astype(BF16), w_exp_down[0].astype(BF16)
    wsg, wsu, wsd = w_sh_gate[0].astype(BF16), w_sh_up[0].astype(BF16), w_sh_down[0].astype(BF16)
    per_step = 4
    assert ne % per_step == 0 and ne <= LANES
    wada = w_ada[0].astype(BF16)
    gpre = norm_pre_mix
    cw = conv_w[0]
    cb = conv_b
    dtb = jnp.pad(dt_bias, ((0, 0), (0, LANES - ssd_heads)))
    alog = jnp.pad(a_log, ((0, 0), (0, LANES - ssd_heads)))
    dskip = jnp.repeat(d_skip, hd, axis=1)

    rows = bp + db
    rows_pad = -(-rows // 8) * 8
    c_all = jnp.pad(jnp.concatenate([c_prompt, c_sample], axis=0), ((0, rows_pad - rows), (0, 0)))
    mod = _modulation(c_all, wada, b_ada)
    mod_p = [mod[:bp, i * d:(i + 1) * d] for i in range(6)]
    mod_s = [mod[bp:rows, i * d:(i + 1) * d] for i in range(6)]

    def trunk(x, mods, c_slot, s_slot, tm, with_kv):
        sh1, sc1 = mods[0], mods[1]
        return _pre_mix(x, sc1, sh1, gpre, win, q_norm, wuq, kv_norm, wuk, wuv_t, c_slot, s_slot,
                        tm=tm, dims=dims, with_kv=with_kv, widths=(ssd_width, nx))

    def ffn(x, attn, y, mods, tm, tm_moe):
        g1, sh2, sc2, g2 = mods[2], mods[3], mods[4], mods[5]
        x1, h2, gates = _post_mix(x, attn, y, g1, sc2, sh2, an, wo_a, wo_y, norm_post_mix, norm_pre_ffn, wr, rb,
                                  tm=tm)
        return _moe(h2, gates, wg, wu, wd, wsg, wsu, wsd, x1, g2, norm_post_ffn, tm=tm_moe, per_step=per_step)

    cos_p, sin_p = _rope_tables(0, seq, rope // 2)
    c_slot, s_slot = _head_slot_tables(cos_p, sin_p, seq, nope, rope)
    xp = x_prompt[0]
    q, k, vt, ckv_p, kr_p, z, xbc, dt = trunk(xp, mod_p, c_slot, s_slot, 512, True)
    attn_p = _prompt_attention(q, k, vt, tq=512)
    conv0 = jnp.zeros((CONV_W - 1, nx), F32)
    ssm0 = jnp.zeros((ssd_heads, hd, ns), F32)
    y_p, ssm_p, conv_tail = _prompt_ssd(z, xbc, dt, conv0, ssm0, cw, cb, dtb, alog, dskip, ssd_norm,
                                        heads=ssd_heads, hd=hd, ns=ns)
    out_p = ffn(xp, attn_p, y_p, mod_p, 512, 512)

    cos_s, sin_s = _rope_tables(past, LANES, rope // 2)
    c_slot_s, s_slot_s = _head_slot_tables(cos_s[:, :1], sin_s[:, :1], db, nope, rope)
    xs = x_sample[:, 0]
    q_s, ckv_s, kr_s, z_s, xbc_s, dt_s = trunk(xs, mod_s, c_slot_s, s_slot_s, db, False)
    kr_s32 = kr_s[:, nope:nope + rope]
    q_abs = _absorb_queries(q_s, w_abs).transpose(1, 0, 2)
    o_lat = _paged_attention(page_table, q_abs, ckv_s, kr_s32, cache_kv_latent[0],
                             cache_k_rope[0].transpose(0, 2, 1), pages_per_step=16, group=2)
    attn_s = _value_up(o_lat.transpose(1, 0, 2), wuv_t)
    y_s, ssm_t, conv_t = _sample_ssd(z_s, xbc_s, dt_s, state_conv[0].transpose(1, 0, 2),
                                     state_ssm[0].transpose(1, 2, 3, 0), cw, cb, dtb, alog, dskip,
                                     ssd_norm, heads=ssd_heads, hd=hd, ns=ns)
    ssm_s = ssm_t.transpose(3, 0, 1, 2)
    conv_s = conv_t.transpose(1, 0, 2)
    out_s = ffn(xs, attn_s, y_s, mod_s, db, db)

    return (out_p[None], out_s[:, None],
            ckv_p[None, None], kr_p[:, nope:nope + rope][None, None],
            ssm_p[None, None], conv_tail[8 - (CONV_W - 1):][None, None],
            ckv_s[None, :, None], kr_s32[None, :, None],
            ssm_s[None], conv_s[None])
```

```python
import functools

import jax
import jax.numpy as jnp
from jax import lax
from jax.experimental import pallas as pl
from jax.experimental.pallas import tpu as pltpu

F32 = jnp.float32
BF16 = jnp.bfloat16

EPS = 1e-6
ROPE_THETA = 10000.0
ROUTED_SCALE = 2.5
N_EXPERT_GROUPS = 8
TOPK_GROUPS = 4
TOP_K = 8
CONV_W = 4
SSD_CHUNK = 128
SSD_GROUPS = 2

LANES = 128
ONES_ROWS = 16
LOG2E = 1.4426950408889634
VMEM_LIMIT = 56 << 20


def _cparams(sem):
    return pltpu.CompilerParams(dimension_semantics=sem, vmem_limit_bytes=VMEM_LIMIT)


def _dot(a, b):
    return jnp.dot(a, b, preferred_element_type=F32)


def _dot_nt(a, b):
    return lax.dot_general(a, b, (((1,), (1,)), ((), ())), preferred_element_type=F32)


def _dot_tn(a, b):
    return lax.dot_general(a, b, (((0,), (0,)), ((), ())), preferred_element_type=F32)


def _rms(x, g):
    return x * lax.rsqrt(jnp.mean(x * x, axis=-1, keepdims=True) + EPS) * g


def _silu(x):
    return x * jax.nn.sigmoid(x)


def _softplus(x):
    return jnp.maximum(x, 0.0) + jnp.log1p(jnp.exp(-jnp.abs(x)))


def _split3(x):
    x1 = x.astype(BF16)
    r = x - x1.astype(F32)
    x2 = r.astype(BF16)
    x3 = (r - x2.astype(F32)).astype(BF16)
    return x1, x2, x3


def _rope_kernel(inv_ref, cos_ref, sin_ref, *, pos0):
    n = cos_ref.shape[1]
    pos = (lax.broadcasted_iota(jnp.int32, (inv_ref.shape[0], n), 1) + pos0).astype(F32)
    ang = pos * inv_ref[...]
    cos_ref[...] = jnp.cos(ang)
    sin_ref[...] = jnp.sin(ang)


def _rope_tables(pos0, n, half):
    inv = (ROPE_THETA ** (-jnp.arange(half, dtype=F32) / half)).reshape(half, 1)
    cos_t, sin_t = pl.pallas_call(
        functools.partial(_rope_kernel, pos0=pos0),
        out_shape=(jax.ShapeDtypeStruct((half, n), F32),) * 2,
        name="rope_tables",
    )(inv)
    return cos_t, sin_t


def _head_slot_tables(cos_t, sin_t, m, nope, rope):
    n = cos_t.shape[1]
    c = jnp.broadcast_to(cos_t.T, (m, cos_t.shape[0])) if n == 1 else cos_t.T
    s = jnp.broadcast_to(sin_t.T, (m, sin_t.shape[0])) if n == 1 else sin_t.T
    pad = LANES - nope - rope
    c_slot = jnp.concatenate([jnp.ones((m, nope), F32), c, c, jnp.zeros((m, pad), F32)], axis=1)
    s_slot = jnp.concatenate([jnp.zeros((m, nope), F32), s, s, jnp.zeros((m, pad), F32)], axis=1)
    return c_slot, s_slot


def _mod_kernel(c_ref, w_ref, b_ref, o_ref):
    o_ref[...] = _dot(_silu(c_ref[...]).astype(BF16), w_ref[...]) + b_ref[...]


def _modulation(c_all, w_ada, b_ada):
    m, d = c_all.shape
    n = w_ada.shape[1]
    tn = 1536
    return pl.pallas_call(
        _mod_kernel,
        grid=(n // tn,),
        in_specs=[pl.BlockSpec((m, d), lambda j: (0, 0)),
                  pl.BlockSpec((d, tn), lambda j: (0, j)),
                  pl.BlockSpec((1, tn), lambda j: (0, j))],
        out_specs=pl.BlockSpec((m, tn), lambda j: (0, j)),
        out_shape=jax.ShapeDtypeStruct((m, n), F32),
        compiler_params=_cparams(("parallel",)),
        name="modulation",
    )(c_all, w_ada, b_ada)


def _pre_kernel(x_ref, sc_ref, sh_ref, gpre_ref, win_ref, qn_ref, wuq_ref, kvn_ref, wuk_ref, wuv_ref,
                c_ref, s_ref, *out_refs, dims, with_kv):
    ql, kvl, heads, scale = dims
    if with_kv:
        q_ref, k_ref, v_ref, ckv_ref, kr_ref, z_ref, xbc_ref, dt_ref = out_refs
    else:
        q_ref, ckv_ref, kr_ref, z_ref, xbc_ref, dt_ref = out_refs
    h = (_rms(x_ref[...], gpre_ref[...]) * (1.0 + sc_ref[...]) + sh_ref[...]).astype(BF16)
    c_slot = c_ref[...]
    s_slot = s_ref[...]
    hw = heads * LANES
    o0 = 0
    cq = _dot(h, win_ref[:, o0:o0 + ql])
    o0 += ql
    qa = _dot(_rms(cq, qn_ref[...]).astype(BF16), wuq_ref[...])
    for hh in range(heads):
        lo = hh * LANES
        qh = qa[:, lo:lo + LANES] * c_slot + qa[:, hw + lo:hw + lo + LANES] * s_slot
        q_ref[hh] = (qh * scale).astype(BF16)
    ckv = _rms(_dot(h, win_ref[:, o0:o0 + kvl]), kvn_ref[...])
    o0 += kvl
    ckv_ref[...] = ckv
    kb = _dot(h, win_ref[:, o0:o0 + 2 * LANES])
    o0 += 2 * LANES
    kr = kb[:, :LANES] * c_slot + kb[:, LANES:] * s_slot
    kr_ref[...] = kr
    if with_kv:
        ckv_b = ckv.astype(BF16)
        ka = _dot(ckv_b, wuk_ref[...])
        for hh in range(heads):
            lo = hh * LANES
            k_ref[hh] = (ka[:, lo:lo + LANES] + kr).astype(BF16)
            vt = _dot_nt(wuv_ref[hh], ckv_b).astype(BF16)
            v_ref[hh] = jnp.concatenate([vt, jnp.ones((v_ref.shape[1] - vt.shape[0], vt.shape[1]), BF16)], axis=0)
    nz = z_ref.shape[1]
    z_ref[...] = _dot(h, win_ref[:, o0:o0 + nz])
    o0 += nz
    nx = xbc_ref.shape[1]
    xbc_ref[...] = _dot(h, win_ref[:, o0:o0 + nx])
    o0 += nx
    dt_ref[...] = _dot(h, win_ref[:, o0:o0 + LANES])


def _pre_mix(x, sc, sh, gpre, win, qn, wuq, kvn, wuk, wuv, c_slot, s_slot, *, tm, dims, with_kv, widths):
    m, d = x.shape
    ql, kvl, heads, _ = dims
    nz, nx = widths
    mm = sc.shape[0]
    row = lambda i: (i, 0)
    fix = lambda i: (0, 0)
    mod_spec = pl.BlockSpec((tm, d), row) if mm == m else pl.BlockSpec((1, d), fix)
    head_spec = pl.BlockSpec((heads, tm, LANES), lambda i: (0, i, 0))
    head_shape = jax.ShapeDtypeStruct((heads, m, LANES), BF16)
    out_specs = [head_spec]
    out_shape = [head_shape]
    if with_kv:
        vrows = wuv.shape[1] + ONES_ROWS
        out_specs += [head_spec, pl.BlockSpec((heads, vrows, tm), lambda i: (0, 0, i))]
        out_shape += [head_shape, jax.ShapeDtypeStruct((heads, vrows, m), BF16)]
    out_specs += [pl.BlockSpec((tm, kvl), row), pl.BlockSpec((tm, LANES), row), pl.BlockSpec((tm, nz), row),
                  pl.BlockSpec((tm, nx), row), pl.BlockSpec((tm, LANES), row)]
    out_shape += [jax.ShapeDtypeStruct((m, kvl), F32), jax.ShapeDtypeStruct((m, LANES), F32),
                  jax.ShapeDtypeStruct((m, nz), F32), jax.ShapeDtypeStruct((m, nx), F32),
                  jax.ShapeDtypeStruct((m, LANES), F32)]
    full = lambda a: pl.BlockSpec(a.shape, lambda i: (0,) * a.ndim)
    return pl.pallas_call(
        functools.partial(_pre_kernel, dims=dims, with_kv=with_kv),
        grid=(m // tm,),
        in_specs=[pl.BlockSpec((tm, d), row), mod_spec, mod_spec, full(gpre), full(win), full(qn), full(wuq),
                  full(kvn), full(wuk), full(wuv), pl.BlockSpec((tm, LANES), row), pl.BlockSpec((tm, LANES), row)],
        out_specs=out_specs,
        out_shape=out_shape,
        compiler_params=_cparams(("parallel",)),
        name="pre_mix_kv" if with_kv else "pre_mix",
    )(x, sc, sh, gpre, win, qn, wuq, kvn, wuk, wuv, c_slot, s_slot)


def _attn_kernel(qi_ref, ki_ref, q_ref, k_ref, vt_ref, o_ref, m_sc, acc_sc, st_sc):
    step = pl.program_id(0)
    qi = qi_ref[step]
    ki = ki_ref[step]
    heads = vt_ref.shape[0]
    vh = vt_ref.shape[1] - ONES_ROWS

    @pl.when(ki == 0)
    def _():
        m_sc[...] = jnp.full(m_sc.shape, -jnp.inf, F32)
        acc_sc[...] = jnp.zeros(acc_sc.shape, F32)

    def update(diagonal):
        st_sc[0] = _dot_nt(k_ref[0], q_ref[0])
        for hh in range(heads):
            if hh + 1 < heads:
                st_sc[(hh + 1) % 2] = _dot_nt(k_ref[hh + 1], q_ref[hh + 1])
            st = st_sc[hh % 2]
            if diagonal:
                kr = lax.broadcasted_iota(jnp.int32, st.shape, 0)
                qc = lax.broadcasted_iota(jnp.int32, st.shape, 1)
                st = jnp.where(kr <= qc, st, -jnp.inf)
            m_prev = m_sc[hh]
            m_new = jnp.maximum(m_prev, jnp.max(st, axis=0, keepdims=True))
            alpha = jnp.exp2(m_prev - m_new)
            p = jnp.exp2(st - m_new).astype(BF16)
            acc_sc[hh] = alpha * acc_sc[hh] + _dot(vt_ref[hh], p)
            m_sc[hh] = m_new

    @pl.when(ki < qi)
    def _():
        update(False)

    @pl.when(ki == qi)
    def _():
        update(True)
        for hh in range(heads):
            acc = acc_sc[hh]
            o_ref[hh * vh:(hh + 1) * vh, :] = (acc[:vh] / acc[vh:vh + 1]).astype(o_ref.dtype)


def _prompt_attention(q, k, vt, *, tq):
    heads, s, _ = q.shape
    vrows = vt.shape[1]
    vh = vrows - ONES_ROWS
    nq = s // tq
    pairs = [(i, j) for i in range(nq) for j in range(i + 1)]
    qi_arr = jnp.asarray([p[0] for p in pairs], jnp.int32)
    ki_arr = jnp.asarray([p[1] for p in pairs], jnp.int32)
    grid_spec = pltpu.PrefetchScalarGridSpec(
        num_scalar_prefetch=2,
        grid=(len(pairs),),
        in_specs=[pl.BlockSpec((heads, tq, LANES), lambda t, qi, ki: (0, qi[t], 0)),
                  pl.BlockSpec((heads, tq, LANES), lambda t, qi, ki: (0, ki[t], 0)),
                  pl.BlockSpec((heads, vrows, tq), lambda t, qi, ki: (0, 0, ki[t]))],
        out_specs=pl.BlockSpec((heads * vh, tq), lambda t, qi, ki: (0, qi[t])),
        scratch_shapes=[pltpu.VMEM((heads, 1, tq), F32), pltpu.VMEM((heads, vrows, tq), F32),
                        pltpu.VMEM((2, tq, tq), F32)],
    )
    return pl.pallas_call(
        _attn_kernel,
        grid_spec=grid_spec,
        out_shape=jax.ShapeDtypeStruct((heads * vh, s), BF16),
        compiler_params=_cparams(("arbitrary",)),
        name="prompt_attention",
    )(qi_arr, ki_arr, q, k, vt)


def _ssd_kernel(z_ref, xbc_ref, dt_ref, conv0_ref, ssm0_ref, cw_ref, cb_ref, dtb_ref, alog_ref, dskip_ref,
                gn_ref, y_ref, ssm_ref, conv_ref, cat_sc, st_sc, *, heads, hd, ns):
    i = pl.program_id(0)
    q = xbc_ref.shape[0]
    width = heads * hd
    tail = CONV_W - 1

    @pl.when(i == 0)
    def _():
        cat_sc[0:8, :] = jnp.zeros((8, cat_sc.shape[1]), F32)
        cat_sc[8 - tail:8, :] = conv0_ref[...]
        st_sc[...] = ssm0_ref[...]

    xbc = xbc_ref[...]
    cat_sc[8:8 + q, :] = xbc
    acc = cb_ref[...]
    for kk in range(CONV_W):
        acc = acc + cat_sc[8 - tail + kk:8 - tail + kk + q, :] * cw_ref[kk:kk + 1, :]
    act = _silu(acc)
    cat_sc[0:8, :] = xbc[q - 8:q, :]
    conv_ref[...] = xbc[q - 8:q, :]

    xs = act[:, :width]
    gb = SSD_GROUPS * ns
    bm = act[:, width:width + gb]
    cm = act[:, width + gb:width + 2 * gb]
    dt = _softplus(dt_ref[...] + dtb_ref[...])
    a = dt * (-jnp.exp(alog_ref[...]))
    r = lax.broadcasted_iota(jnp.int32, (q, q), 0)
    c = lax.broadcasted_iota(jnp.int32, (q, q), 1)
    causal = c <= r
    tril = jnp.where(causal, 1.0, 0.0).astype(BF16)
    a1, a2, a3 = _split3(a)
    a_cum = _dot(tril, a1) + _dot(tril, a2) + _dot(tril, a3)
    a_cum_t = a_cum.T
    cb = [_dot_nt(cm[:, g * ns:(g + 1) * ns].astype(BF16), bm[:, g * ns:(g + 1) * ns].astype(BF16))
          for g in range(SSD_GROUPS)]
    rep = heads // SSD_GROUPS
    ys = []
    for hh in range(heads):
        g = hh // rep
        acol = a_cum[:, hh:hh + 1]
        arow = a_cum_t[hh:hh + 1, :]
        decay = jnp.exp(jnp.where(causal, acol - arow, -jnp.inf))
        xh = xs[:, hh * hd:(hh + 1) * hd]
        xdt = xh * dt[:, hh:hh + 1]
        y_diag = _dot((cb[g] * decay).astype(BF16), xdt.astype(BF16))
        st = st_sc[hh]
        c_g = cm[:, g * ns:(g + 1) * ns].astype(BF16)
        y_off = _dot_nt(c_g, st.astype(BF16)) * jnp.exp(acol)
        ys.append(y_diag + y_off + dskip_ref[:, hh * hd:(hh + 1) * hd] * xh)
        a_last = a_cum[q - 1:q, hh:hh + 1]
        to_end = jnp.exp(a_last - acol)
        upd = _dot_tn((xdt * to_end).astype(BF16), bm[:, g * ns:(g + 1) * ns].astype(BF16))
        st_sc[hh] = st * jnp.exp(a_last) + upd
    y = jnp.concatenate(ys, axis=1) * _silu(z_ref[...])
    gs = width // SSD_GROUPS
    y = jnp.concatenate([_rms(y[:, g * gs:(g + 1) * gs], gn_ref[:, g * gs:(g + 1) * gs])
                         for g in range(SSD_GROUPS)], axis=1)
    y_ref[...] = y.astype(y_ref.dtype)
    ssm_ref[...] = st_sc[...]


def _prompt_ssd(z, xbc, dt, conv0, ssm0, cw, cb, dtb, alog, dskip, gn, *, heads, hd, ns):
    s, width = z.shape
    nx = xbc.shape[1]
    q = SSD_CHUNK
    row = lambda i: (i, 0)
    fix2 = lambda i: (0, 0)
    fix3 = lambda i: (0, 0, 0)
    full = lambda a: pl.BlockSpec(a.shape, fix2 if a.ndim == 2 else fix3)
    return pl.pallas_call(
        functools.partial(_ssd_kernel, heads=heads, hd=hd, ns=ns),
        grid=(s // q,),
        in_specs=[pl.BlockSpec((q, width), row), pl.BlockSpec((q, nx), row), pl.BlockSpec((q, LANES), row),
                  full(conv0), full(ssm0), full(cw), full(cb), full(dtb), full(alog), full(dskip), full(gn)],
        out_specs=[pl.BlockSpec((q, width), row), pl.BlockSpec((heads, hd, ns), fix3),
                   pl.BlockSpec((8, nx), fix2)],
        out_shape=[jax.ShapeDtypeStruct((s, width), BF16), jax.ShapeDtypeStruct((heads, hd, ns), F32),
                   jax.ShapeDtypeStruct((8, nx), F32)],
        scratch_shapes=[pltpu.VMEM((8 + q, nx), F32), pltpu.VMEM((heads, hd, ns), F32)],
        compiler_params=_cparams(("arbitrary",)),
        name="prompt_ssd",
    )(z, xbc, dt, conv0, ssm0, cw, cb, dtb, alog, dskip, gn)


def _ssd_step_kernel(z_ref, xbc_ref, dt_ref, conv_ref, ssm_ref, cw_ref, cb_ref, dtb_ref, alog_ref, dskip_ref,
                     gn_ref, y_ref, ssm_out, conv_out, xs_sc, xdt_sc, b_sc, c_sc, da_sc, y_sc, *, heads, hd, ns):
    hh = pl.program_id(0)
    width = heads * hd
    gb = SSD_GROUPS * ns

    @pl.when(hh == 0)
    def _():
        xn = xbc_ref[...]
        acc = cb_ref[...] + xn * cw_ref[CONV_W - 1:CONV_W, :]
        for kk in range(CONV_W - 1):
            acc = acc + conv_ref[kk] * cw_ref[kk:kk + 1, :]
        for kk in range(CONV_W - 2):
            conv_out[kk] = conv_ref[kk + 1]
        conv_out[CONV_W - 2] = xn
        act = _silu(acc)
        xs = act[:, :width]
        xs_sc[...] = xs
        dt = _softplus(dt_ref[...] + dtb_ref[...])
        da_sc[...] = jnp.exp(dt * (-jnp.exp(alog_ref[...]))).T
        head_of_lane = lax.broadcasted_iota(jnp.int32, (1, width), 1) // hd
        dt_lane = jnp.zeros(xs.shape, F32)
        for h2 in range(heads):
            dt_lane = jnp.where(head_of_lane == h2, dt[:, h2:h2 + 1], dt_lane)
        xdt_sc[...] = (xs * dt_lane).T
        b_sc[...] = act[:, width:width + gb].T
        c_sc[...] = act[:, width + gb:width + 2 * gb].T

    g = hh // (heads // SSD_GROUPS)
    goff = pl.multiple_of(g * ns, ns)
    b_g = b_sc[pl.ds(goff, ns), :]
    c_g = c_sc[pl.ds(goff, ns), :]
    da = da_sc[pl.ds(hh, 1), :]
    row0 = pl.multiple_of(hh * hd, hd)

    def body(p, carry):
        xdt = xdt_sc[pl.ds(row0 + p, 1), :]
        new = ssm_ref[0, p] * da + xdt * b_g
        ssm_out[0, p] = new
        y_sc[pl.ds(row0 + p, 1), :] = jnp.sum(new * c_g, axis=0, keepdims=True)
        return carry

    lax.fori_loop(0, hd, body, 0, unroll=8)

    @pl.when(hh == heads - 1)
    def _():
        xs = xs_sc[...]
        y = (y_sc[...].T + dskip_ref[...] * xs) * _silu(z_ref[...])
        gs = width // SSD_GROUPS
        y = jnp.concatenate([_rms(y[:, k * gs:(k + 1) * gs], gn_ref[:, k * gs:(k + 1) * gs])
                             for k in range(SSD_GROUPS)], axis=1)
        y_ref[...] = y.astype(y_ref.dtype)


def _sample_ssd(z, xbc, dt, conv_t, ssm_t, cw, cb, dtb, alog, dskip, gn, *, heads, hd, ns):
    b, width = z.shape
    gb = SSD_GROUPS * ns
    full = lambda a: pl.BlockSpec(a.shape, lambda h: (0,) * a.ndim)
    per_head = pl.BlockSpec((1, hd, ns, b), lambda h: (h, 0, 0, 0))
    return pl.pallas_call(
        functools.partial(_ssd_step_kernel, heads=heads, hd=hd, ns=ns),
        grid=(heads,),
        in_specs=[full(z), full(xbc), full(dt), full(conv_t), per_head,
                  full(cw), full(cb), full(dtb), full(alog), full(dskip), full(gn)],
        out_specs=[pl.BlockSpec((b, width), lambda h: (0, 0)), per_head,
                   pl.BlockSpec(conv_t.shape, lambda h: (0, 0, 0))],
        out_shape=[jax.ShapeDtypeStruct((b, width), BF16), jax.ShapeDtypeStruct(ssm_t.shape, F32),
                   jax.ShapeDtypeStruct(conv_t.shape, F32)],
        scratch_shapes=[pltpu.VMEM((b, width), F32), pltpu.VMEM((width, b), F32), pltpu.VMEM((gb, b), F32),
                        pltpu.VMEM((gb, b), F32), pltpu.VMEM((LANES, b), F32), pltpu.VMEM((width, b), F32)],
        compiler_params=_cparams(("arbitrary",)),
        name="sample_ssd",
    )(z, xbc, dt, conv_t, ssm_t, cw, cb, dtb, alog, dskip, gn)


def _absorb_kernel(q_ref, w_ref, o_ref):
    o_ref[0] = _dot(q_ref[0], w_ref[0]).astype(o_ref.dtype)


def _absorb_queries(q, w_abs):
    heads, b, _ = q.shape
    n = w_abs.shape[2]
    per = lambda h: (h, 0, 0)
    return pl.pallas_call(
        _absorb_kernel,
        grid=(heads,),
        in_specs=[pl.BlockSpec((1, b, LANES), per), pl.BlockSpec((1, LANES, n), per)],
        out_specs=pl.BlockSpec((1, b, n), per),
        out_shape=jax.ShapeDtypeStruct((heads, b, n), BF16),
        compiler_params=_cparams(("parallel",)),
        name="absorb_queries",
    )(q, w_abs)


def _paged_attn_kernel(pt_ref, q_ref, lnew_ref, knew_ref, lat_hbm, krt_hbm, o_ref,
                       lat_buf, kr_buf, sem, m_sc, l_sc, acc_sc, *, pages_per_step, kvl, rope):
    b = pl.program_id(0)
    c = pl.program_id(1)
    nb = pl.num_programs(0)
    nc = pl.num_programs(1)
    t = b * nc + c
    slot = t % 2
    page = lat_hbm.shape[1]
    group = q_ref.shape[0]

    def copies(bb, cc, sl):
        out = []
        for g in range(group):
            for j in range(pages_per_step):
                pg = pt_ref[bb * group + g, cc * pages_per_step + j]
                out.append(pltpu.make_async_copy(lat_hbm.at[pg], lat_buf.at[sl, g, pl.ds(j * page, page)],
                                                 sem.at[0, sl]))
                out.append(pltpu.make_async_copy(krt_hbm.at[pg], kr_buf.at[sl, g, :, pl.ds(j * page, page)],
                                                 sem.at[1, sl]))
        return out

    @pl.when(t == 0)
    def _():
        for cp in copies(b, c, slot):
            cp.start()

    @pl.when(t + 1 < nb * nc)
    def _():
        wrap = c + 1 == nc
        nb_ = jnp.where(wrap, b + 1, b)
        nc_ = jnp.where(wrap, 0, c + 1)
        for cp in copies(nb_, nc_, 1 - slot):
            cp.start()

    @pl.when(c == 0)
    def _():
        m_sc[...] = jnp.full(m_sc.shape, -jnp.inf, F32)
        l_sc[...] = jnp.zeros(l_sc.shape, F32)
        acc_sc[...] = jnp.zeros(acc_sc.shape, F32)

    for cp in copies(b, c, slot):
        cp.wait()

    for g in range(group):
        q = q_ref[g]
        lat = lat_buf[slot, g].astype(BF16)
        krt = kr_buf[slot, g].astype(BF16)
        s = _dot_nt(q[:, :kvl], lat) + _dot(q[:, kvl:kvl + rope], krt)
        m_prev = m_sc[g]
        m_new = jnp.maximum(m_prev, jnp.max(s, axis=-1, keepdims=True))
        alpha = jnp.exp(m_prev - m_new)
        p = jnp.exp(s - m_new)
        l_sc[g] = alpha * l_sc[g] + jnp.sum(p, axis=-1, keepdims=True)
        acc_sc[g] = alpha * acc_sc[g] + _dot(p.astype(BF16), lat)
        m_sc[g] = m_new

    @pl.when(c == nc - 1)
    def _():
        for g in range(group):
            q = q_ref[g].astype(F32)
            lnew = lnew_ref[g]
            knew = knew_ref[g]
            s_new = (jnp.sum(q[:, :kvl] * lnew, axis=-1, keepdims=True)
                     + jnp.sum(q[:, kvl:kvl + rope] * knew, axis=-1, keepdims=True))
            m_prev = m_sc[g]
            m_fin = jnp.maximum(m_prev, s_new)
            alpha = jnp.exp(m_prev - m_fin)
            p_new = jnp.exp(s_new - m_fin)
            l_fin = alpha * l_sc[g] + p_new
            o_ref[g] = (alpha * acc_sc[g] + p_new * lnew) / l_fin


def _paged_attention(page_table, q_abs, lat_new, kr_new, cache_lat, cache_krt, *, pages_per_step, group):
    b, heads, qw = q_abs.shape
    n_pages = page_table.shape[1]
    page, kvl = cache_lat.shape[1:]
    rope = cache_krt.shape[1]
    nc = n_pages // pages_per_step
    rows = pages_per_step * page
    per = lambda i, j, pt: (i, 0, 0)
    grid_spec = pltpu.PrefetchScalarGridSpec(
        num_scalar_prefetch=1,
        grid=(b // group, nc),
        in_specs=[pl.BlockSpec((group, heads, qw), per), pl.BlockSpec((group, 1, kvl), per),
                  pl.BlockSpec((group, 1, rope), per),
                  pl.BlockSpec(memory_space=pl.ANY), pl.BlockSpec(memory_space=pl.ANY)],
        out_specs=pl.BlockSpec((group, heads, kvl), per),
        scratch_shapes=[pltpu.VMEM((2, group, rows, kvl), F32), pltpu.VMEM((2, group, rope, rows), F32),
                        pltpu.SemaphoreType.DMA((2, 2)),
                        pltpu.VMEM((group, heads, 1), F32), pltpu.VMEM((group, heads, 1), F32),
                        pltpu.VMEM((group, heads, kvl), F32)],
    )
    return pl.pallas_call(
        functools.partial(_paged_attn_kernel, pages_per_step=pages_per_step, kvl=kvl, rope=rope),
        grid_spec=grid_spec,
        out_shape=jax.ShapeDtypeStruct((b, heads, kvl), F32),
        compiler_params=_cparams(("arbitrary", "arbitrary")),
        name="paged_attention",
    )(page_table, q_abs, lat_new.reshape(b, 1, kvl), kr_new.reshape(b, 1, rope), cache_lat, cache_krt)


def _value_up_kernel(o_ref, w_ref, a_ref):
    a_ref[...] = _dot_nt(w_ref[0], o_ref[0].astype(BF16)).astype(a_ref.dtype)


def _value_up(o_lat, wuv_t):
    heads, b, kvl = o_lat.shape
    vh = wuv_t.shape[1]
    per = lambda h: (h, 0, 0)
    return pl.pallas_call(
        _value_up_kernel,
        grid=(heads,),
        in_specs=[pl.BlockSpec((1, b, kvl), per), pl.BlockSpec((1, vh, kvl), per)],
        out_specs=pl.BlockSpec((vh, b), lambda h: (h, 0)),
        out_shape=jax.ShapeDtypeStruct((heads * vh, b), BF16),
        compiler_params=_cparams(("parallel",)),
        name="value_up",
    )(o_lat, wuv_t)


def _post_kernel(x_ref, attn_ref, y_ref, g1_ref, sc_ref, sh_ref, an_ref, wo_a_ref, wo_y_ref, npost_ref, npre_ref,
                 wr_ref, rb_ref, x1_ref, h2_ref, gates_ref):
    a = attn_ref[...].astype(F32)
    an = a * lax.rsqrt(jnp.mean(a * a, axis=0, keepdims=True) + EPS) * an_ref[...]
    mix = _dot_tn(an.astype(BF16), wo_a_ref[...]) + _dot(y_ref[...], wo_y_ref[...])
    x1 = x_ref[...] + g1_ref[...] * _rms(mix, npost_ref[...])
    x1_ref[...] = x1
    h2 = _rms(x1, npre_ref[...]) * (1.0 + sc_ref[...]) + sh_ref[...]
    h_hi = h2.astype(BF16)
    h2_ref[...] = h_hi
    h_lo = (h2 - h_hi.astype(F32)).astype(BF16)
    logits = _dot_nt(wr_ref[0], h_hi) + _dot_nt(wr_ref[1], h_hi) + _dot_nt(wr_ref[0], h_lo)
    ne, tm = logits.shape
    s = jax.nn.sigmoid(logits)
    sel = s + rb_ref[...]
    per = ne // N_EXPERT_GROUPS
    neg = -jnp.inf
    sel3 = sel.reshape(N_EXPERT_GROUPS, per, tm)
    i3 = lax.broadcasted_iota(jnp.int32, sel3.shape, 1)
    top1 = jnp.max(sel3, axis=1, keepdims=True)
    first = jnp.min(jnp.where(sel3 == top1, i3, per), axis=1, keepdims=True)
    top2 = jnp.max(jnp.where(i3 == first, neg, sel3), axis=1, keepdims=True)
    gscore = (top1 + top2).reshape(N_EXPERT_GROUPS, tm)
    ig = lax.broadcasted_iota(jnp.int32, gscore.shape, 0)
    gmask = jnp.zeros(gscore.shape, F32)
    for _ in range(TOPK_GROUPS):
        best = jnp.max(gscore, axis=0, keepdims=True)
        pick = ig == jnp.min(jnp.where(gscore == best, ig, N_EXPERT_GROUPS), axis=0, keepdims=True)
        gmask = jnp.where(pick, 1.0, gmask)
        gscore = jnp.where(pick, neg, gscore)
    emask = jnp.concatenate([jnp.broadcast_to(gmask[g:g + 1, :], (per, tm)) for g in range(N_EXPERT_GROUPS)], axis=0)
    cand = jnp.where(emask > 0.5, sel, neg)
    ie = lax.broadcasted_iota(jnp.int32, cand.shape, 0)
    chosen = jnp.zeros(cand.shape, F32)
    for _ in range(TOP_K):
        best = jnp.max(cand, axis=0, keepdims=True)
        pick = ie == jnp.min(jnp.where(cand == best, ie, ne), axis=0, keepdims=True)
        chosen = jnp.where(pick, 1.0, chosen)
        cand = jnp.where(pick, neg, cand)
    w_sel = chosen * s
    gates = w_sel / (jnp.sum(w_sel, axis=0, keepdims=True) + 1e-20) * ROUTED_SCALE
    gates_ref[...] = jnp.concatenate([gates, jnp.zeros((LANES - ne, tm), F32)], axis=0).T


def _post_mix(x, attn_t, y, g1, sc, sh, an, wo_a, wo_y, npost, npre, wr, rb, *, tm):
    m, d = x.shape
    mm = g1.shape[0]
    row = lambda i: (i, 0)
    fix = lambda i: (0,) * 2
    full = lambda a: pl.BlockSpec(a.shape, (lambda i: (0,) * a.ndim))
    mod_spec = pl.BlockSpec((tm, d), row) if mm == m else pl.BlockSpec((1, d), fix)
    return pl.pallas_call(
        _post_kernel,
        grid=(m // tm,),
        in_specs=[pl.BlockSpec((tm, d), row), pl.BlockSpec((attn_t.shape[0], tm), lambda i: (0, i)),
                  pl.BlockSpec((tm, y.shape[1]), row), mod_spec, mod_spec, mod_spec,
                  full(an), full(wo_a), full(wo_y), full(npost), full(npre), full(wr), full(rb)],
        out_specs=[pl.BlockSpec((tm, d), row), pl.BlockSpec((tm, d), row), pl.BlockSpec((tm, LANES), row)],
        out_shape=[jax.ShapeDtypeStruct((m, d), F32), jax.ShapeDtypeStruct((m, d), BF16),
                   jax.ShapeDtypeStruct((m, LANES), F32)],
        compiler_params=_cparams(("parallel",)),
        name="post_mix",
    )(x, attn_t, y, g1, sc, sh, an, wo_a, wo_y, npost, npre, wr, rb)


def _moe_kernel(h_ref, gates_ref, wg_ref, wu_ref, wd_ref, wsg_ref, wsu_ref, wsd_ref, x1_ref, g2_ref, npost_ref,
                o_ref, acc_sc):
    g = pl.program_id(1)
    per_step = wg_ref.shape[0]
    h = h_ref[...]

    @pl.when(g == 0)
    def _():
        hid = _silu(_dot(h, wsg_ref[...])) * _dot(h, wsu_ref[...])
        acc_sc[...] = _dot(hid.astype(BF16), wsd_ref[...])

    gates = pltpu.roll(gates_ref[...], (LANES - g * per_step) % LANES, 1)
    hids = []
    for j in range(per_step):
        hid = _silu(_dot(h, wg_ref[j])) * _dot(h, wu_ref[j])
        hids.append((hid * gates[:, j:j + 1]).astype(BF16))
    acc_sc[...] += _dot(jnp.concatenate(hids, axis=1), wd_ref[0])

    @pl.when(g == pl.num_programs(1) - 1)
    def _():
        o_ref[...] = x1_ref[...] + g2_ref[...] * _rms(acc_sc[...], npost_ref[...])


def _moe(h2, gates, wg, wu, wd, wsg, wsu, wsd, x1, g2, npost, *, tm, per_step):
    m, d = h2.shape
    ne, _, ff = wg.shape
    ng = ne // per_step
    mm = g2.shape[0]
    wd_g = wd.reshape(ng, per_step * ff, d)
    row = lambda i, g: (i, 0)
    fix = lambda i, g: (0, 0)
    grp = lambda i, g: (g, 0, 0)
    mod_spec = pl.BlockSpec((tm, d), row) if mm == m else pl.BlockSpec((1, d), fix)
    full = lambda a: pl.BlockSpec(a.shape, fix)
    return pl.pallas_call(
        _moe_kernel,
        grid=(m // tm, ng),
        in_specs=[pl.BlockSpec((tm, d), row), pl.BlockSpec((tm, LANES), row),
                  pl.BlockSpec((per_step, d, ff), grp), pl.BlockSpec((per_step, d, ff), grp),
                  pl.BlockSpec((1, per_step * ff, d), grp), full(wsg), full(wsu), full(wsd),
                  pl.BlockSpec((tm, d), row), mod_spec, full(npost)],
        out_specs=pl.BlockSpec((tm, d), row),
        out_shape=jax.ShapeDtypeStruct((m, d), F32),
        scratch_shapes=[pltpu.VMEM((tm, d), F32)],
        compiler_params=_cparams(("parallel", "arbitrary")),
        name="moe_ffn",
    )(h2, gates, wg, wu, wd_g, wsg, wsu, wsd, x1, g2, npost)


def _head_slots(w, lo_pad):
    rows, heads, n = w.shape
    out = jnp.zeros((rows, heads, LANES), w.dtype)
    out = out.at[:, :, lo_pad:lo_pad + n].set(w)
    return out.reshape(rows, heads * LANES)


def _rot_half(w):
    half = w.shape[-1] // 2
    return jnp.concatenate([-w[..., half:], w[..., :half]], axis=-1)


def kernel(x_prompt, x_sample, c_prompt, c_sample, cache_kv_latent, cache_k_rope, page_table, state_ssm, state_conv, w_ada, b_ada, norm_pre_mix, norm_post_mix, norm_pre_ffn, norm_post_ffn, w_in, q_norm, w_uq, kv_norm, w_uk, w_uv, attn_out_norm, conv_w, conv_b, dt_bias, a_log, d_skip, ssd_norm, w_out, w_router, router_bias, w_exp_gate, w_exp_up, w_exp_down, w_sh_gate, w_sh_up, w_sh_down):
    depth = w_in.shape[0]
    assert depth == 1, "one trunk layer"
    bp, seq, d = x_prompt.shape
    assert bp == 1
    db, dec_seq, _ = x_sample.shape
    assert dec_seq == 1
    ql = q_norm.shape[1]
    kvl = kv_norm.shape[1]
    _, _, heads, qk = w_uq.shape
    nope = w_uk.shape[3]
    rope = qk - nope
    vh = w_uv.shape[3]
    ssd_heads = dt_bias.shape[1]
    ssd_width = ssd_norm.shape[1]
    hd = ssd_width // ssd_heads
    nx = conv_w.shape[2]
    ns = (nx - ssd_width) // (2 * SSD_GROUPS)
    ne = w_router.shape[2]
    mla_width = heads * vh
    n_pages, page = page_table.shape[1], cache_kv_latent.shape[2]
    past = n_pages * page
    scale = float(qk) ** -0.5
    dims_s = (ql, kvl, heads, scale)
    dims_p = (ql, kvl, heads, scale * LOG2E)

    o = 0
    w = w_in[0]
    w_cq = w[:, o:o + ql]; o += ql
    w_ckv = w[:, o:o + kvl]; o += kvl
    w_kr = w[:, o:o + rope]; o += rope
    w_z = w[:, o:o + ssd_width]; o += ssd_width
    w_xbc = w[:, o:o + nx]; o += nx
    w_dt = w[:, o:o + ssd_heads]
    slot = lambda a: jnp.pad(a, ((0, 0), (nope, LANES - nope - rope)))
    win = jnp.concatenate([w_cq, w_ckv, slot(w_kr), slot(_rot_half(w_kr)), w_z, w_xbc,
                           jnp.pad(w_dt, ((0, 0), (0, LANES - ssd_heads)))], axis=1).astype(BF16)
    uq = w_uq[0]
    uq_rot = jnp.concatenate([jnp.zeros_like(uq[..., :nope]), _rot_half(uq[..., nope:])], axis=-1)
    wuq = jnp.concatenate([_head_slots(uq, 0), _head_slots(uq_rot, 0)], axis=1).astype(BF16)
    wuk = _head_slots(w_uk[0], 0).astype(BF16)
    wuv_t = w_uv[0].transpose(1, 2, 0).astype(BF16)
    w_abs = jnp.zeros((heads, LANES, kvl + LANES), F32)
    w_abs = w_abs.at[:, :nope, :kvl].set(w_uk[0].transpose(1, 2, 0))
    w_abs = w_abs.at[:, nope:nope + rope, kvl:kvl + rope].set(jnp.broadcast_to(jnp.eye(rope, dtype=F32), (heads, rope, rope)))
    w_abs = w_abs.astype(BF16)
    an = attn_out_norm[0].reshape(mla_width, 1)
    wo = w_out[0]
    wo_a = wo[:mla_width].astype(BF16)
    wo_y = wo[mla_width:].astype(BF16)
    wr_t = w_router[0].T
    wr_hi = wr_t.astype(BF16)
    wr = jnp.stack([wr_hi, (wr_t - wr_hi.astype(F32)).astype(BF16)])
    rb = router_bias[0].reshape(ne, 1)
    wg, wu, wd = w_exp_gate[0].astype(BF16), w_exp_up[0].astype(BF16), w_exp_down[0].astype(BF16)
    wsg, wsu, wsd = w_sh_gate[0].astype(BF16), w_sh_up[0].astype(BF16), w_sh_down[0].astype(BF16)
    per_step = 8
    assert ne % per_step == 0 and ne <= LANES
    wada = w_ada[0].astype(BF16)
    gpre = norm_pre_mix
    cw = conv_w[0]
    cb = conv_b
    dtb = jnp.pad(dt_bias, ((0, 0), (0, LANES - ssd_heads)))
    alog = jnp.pad(a_log, ((0, 0), (0, LANES - ssd_heads)))
    dskip = jnp.repeat(d_skip, hd, axis=1)

    rows = bp + db
    rows_pad = -(-rows // 8) * 8
    c_all = jnp.pad(jnp.concatenate([c_prompt, c_sample], axis=0), ((0, rows_pad - rows), (0, 0)))
    mod = _modulation(c_all, wada, b_ada)
    mod_p = [mod[:bp, i * d:(i + 1) * d] for i in range(6)]
    mod_s = [mod[bp:rows, i * d:(i + 1) * d] for i in range(6)]

    def trunk(x, mods, c_slot, s_slot, tm, with_kv):
        sh1, sc1 = mods[0], mods[1]
        return _pre_mix(x, sc1, sh1, gpre, win, q_norm, wuq, kv_norm, wuk, wuv_t, c_slot, s_slot,
                        tm=tm, dims=dims_p if with_kv else dims_s, with_kv=with_kv, widths=(ssd_width, nx))

    def ffn(x, attn, y, mods, tm, tm_moe):
        g1, sh2, sc2, g2 = mods[2], mods[3], mods[4], mods[5]
        x1, h2, gates = _post_mix(x, attn, y, g1, sc2, sh2, an, wo_a, wo_y, norm_post_mix, norm_pre_ffn, wr, rb,
                                  tm=tm)
        return _moe(h2, gates, wg, wu, wd, wsg, wsu, wsd, x1, g2, norm_post_ffn, tm=tm_moe, per_step=per_step)

    cos_p, sin_p = _rope_tables(0, seq, rope // 2)
    c_slot, s_slot = _head_slot_tables(cos_p, sin_p, seq, nope, rope)
    xp = x_prompt[0]
    q, k, vt, ckv_p, kr_p, z, xbc, dt = trunk(xp, mod_p, c_slot, s_slot, 512, True)
    attn_p = _prompt_attention(q, k, vt, tq=512)
    conv0 = jnp.zeros((CONV_W - 1, nx), F32)
    ssm0 = jnp.zeros((ssd_heads, hd, ns), F32)
    y_p, ssm_p, conv_tail = _prompt_ssd(z, xbc, dt, conv0, ssm0, cw, cb, dtb, alog, dskip, ssd_norm,
                                        heads=ssd_heads, hd=hd, ns=ns)
    out_p = ffn(xp, attn_p, y_p, mod_p, 512, 512)

    cos_s, sin_s = _rope_tables(past, LANES, rope // 2)
    c_slot_s, s_slot_s = _head_slot_tables(cos_s[:, :1], sin_s[:, :1], db, nope, rope)
    xs = x_sample[:, 0]
    q_s, ckv_s, kr_s, z_s, xbc_s, dt_s = trunk(xs, mod_s, c_slot_s, s_slot_s, db, False)
    kr_s32 = kr_s[:, nope:nope + rope]
    q_abs = _absorb_queries(q_s, w_abs).transpose(1, 0, 2)
    o_lat = _paged_attention(page_table, q_abs, ckv_s, kr_s32, cache_kv_latent[0],
                             cache_k_rope[0].transpose(0, 2, 1), pages_per_step=32, group=2)
    attn_s = _value_up(o_lat.transpose(1, 0, 2), wuv_t)
    y_s, ssm_t, conv_t = _sample_ssd(z_s, xbc_s, dt_s, state_conv[0].transpose(1, 0, 2),
                                     state_ssm[0].transpose(1, 2, 3, 0), cw, cb, dtb, alog, dskip,
                                     ssd_norm, heads=ssd_heads, hd=hd, ns=ns)
    ssm_s = ssm_t.transpose(3, 0, 1, 2)
    conv_s = conv_t.transpose(1, 0, 2)
    out_s = ffn(xs, attn_s, y_s, mod_s, db, db)

    return (out_p[None], out_s[:, None],
            ckv_p[None, None], kr_p[:, nope:nope + rope][None, None],
            ssm_p[None, None], conv_tail[8 - (CONV_W - 1):][None, None],
            ckv_s[None, :, None], kr_s32[None, :, None],
            ssm_s[None], conv_s[None])
```
